```python
import math
import jax, jax.numpy as jnp
from jax import lax
import numpy as np

D_MODEL = 1024
BATCH = 2
SEQ = 8192
DEPTH = 4

HEAD_DIM = 64
BLOCK = 128
ROPE_THETA = 10000.0
EPS = 1e-6
NEG = -1e30
N_BRANCH = 3
BRANCH_WIDTH = D_MODEL // 4

SB_HEADS = BRANCH_WIDTH // HEAD_DIM

MLA_HEADS = 4
MLA_V = BRANCH_WIDTH // MLA_HEADS
MLA_NOPE = 64
MLA_ROPE = 32
MLA_Q_LORA = D_MODEL // 4
MLA_KV_LORA = D_MODEL // 8

DIL_CONFIGS = ((128, 1), (512, 4), (2048, 16))
DIL_HEADS = BRANCH_WIDTH // HEAD_DIM

SB_COLS = 3 * SB_HEADS * HEAD_DIM
MLA_COLS = MLA_Q_LORA + MLA_KV_LORA + MLA_ROPE
DIL_COLS = len(DIL_CONFIGS) * 3 * DIL_HEADS * HEAD_DIM
GATE_COLS = N_BRANCH * D_MODEL
IN_COLS = SB_COLS + MLA_COLS + DIL_COLS + GATE_COLS

N_GROUPS = 4
EXPERTS_PER_GROUP = 4
N_EXPERTS = N_GROUPS * EXPERTS_PER_GROUP
TOP_K = 2
D_EXPERT = D_MODEL // 4

kernel_name = "hybrid_sb_mla_dilated_hmoe"


def rmsnorm(x, g):
    xf = x.astype(jnp.float32)
    y = xf * lax.rsqrt(jnp.mean(xf * xf, axis=-1, keepdims=True) + EPS)
    return (y * g.astype(jnp.float32)).astype(x.dtype)


def apply_rope(x, pos):
    dim = x.shape[-1]
    inv_freq = ROPE_THETA ** (-jnp.arange(0, dim, 2, dtype=jnp.float32) / dim)
    ang = pos.astype(jnp.float32)[:, :, None, None] * inv_freq
    cos, sin = jnp.cos(ang), jnp.sin(ang)
    x1, x2 = jnp.split(x.astype(jnp.float32), 2, axis=-1)
    return jnp.concatenate([x1 * cos - x2 * sin, x1 * sin + x2 * cos], axis=-1).astype(x.dtype)


def query_blocks(t):
    b, s, h, d = t.shape
    return t.reshape(b, s // BLOCK, BLOCK, h, d).transpose(1, 0, 2, 3, 4)


def merge_blocks(t):
    nb, b, q, h, d = t.shape
    return t.transpose(1, 0, 2, 3, 4).reshape(b, nb * q, h, d)


def stick_breaking_attention(q, k, v):
    b, s, h, dh = q.shape
    scale = dh ** -0.5
    key_pos = jnp.arange(s)
    starts = jnp.arange(s // BLOCK) * BLOCK

    def one_block(args):
        q_blk, start = args
        z = jnp.einsum('bqhd,bkhd->bhqk', q_blk, k, preferred_element_type=jnp.float32) * scale
        qpos = start + jnp.arange(BLOCK)
        strict = key_pos[None, :] < qpos[:, None]
        log_beta = jax.nn.log_sigmoid(z)
        log_one_minus = jnp.where(strict, jax.nn.log_sigmoid(-z), 0.0)
        after = lax.cumsum(log_one_minus, axis=3, reverse=True) - log_one_minus
        w = jnp.where(strict, jnp.exp(log_beta + after), 0.0)
        return jnp.einsum('bhqk,bkhd->bqhd', w.astype(v.dtype), v)

    return merge_blocks(lax.map(one_block, (query_blocks(q), starts)))


def mla_attention(c_q, c_kv, k_rope_raw, pos, g_q_lat, w_uq, g_kv_lat, w_ukv):
    b, s, _ = c_q.shape
    q = (rmsnorm(c_q, g_q_lat) @ w_uq).reshape(b, s, MLA_HEADS, MLA_NOPE + MLA_ROPE)
    q_nope = q[..., :MLA_NOPE]
    q_rope = apply_rope(q[..., MLA_NOPE:], pos)
    kv = (rmsnorm(c_kv, g_kv_lat) @ w_ukv).reshape(b, s, MLA_HEADS, MLA_NOPE + MLA_V)
    k_nope, v = kv[..., :MLA_NOPE], kv[..., MLA_NOPE:]
    k_rope = apply_rope(k_rope_raw[:, :, None, :], pos)[:, :, 0]
    scale = (MLA_NOPE + MLA_ROPE) ** -0.5
    key_pos = jnp.arange(s)
    starts = jnp.arange(s // BLOCK) * BLOCK

    def one_block(args):
        qn, qr, start = args
        sc = (jnp.einsum('bqhd,bkhd->bhqk', qn, k_nope, preferred_element_type=jnp.float32)
              + jnp.einsum('bqhd,bkd->bhqk', qr, k_rope, preferred_element_type=jnp.float32)) * scale
        qpos = start + jnp.arange(BLOCK)
        causal = key_pos[None, :] <= qpos[:, None]
        p = jax.nn.softmax(jnp.where(causal, sc, NEG), axis=-1)
        return jnp.einsum('bhqk,bkhd->bqhd', p.astype(v.dtype), v)

    return merge_blocks(lax.map(one_block, (query_blocks(q_nope), query_blocks(q_rope), starts)))


def dilated_group_attention(q, k, v, window, dilation):
    b, s, h, dh = q.shape
    L = window // dilation
    s_pad = -(-s // window) * window
    n = s_pad // dilation
    nb = n // L

    def strided_blocks(t):
        t = jnp.pad(t, ((0, 0), (0, s_pad - s), (0, 0), (0, 0)))
        t = t.reshape(b, n, dilation, h, dh).transpose(0, 2, 1, 3, 4)
        return t.reshape(b, dilation, nb, L, h, dh)

    qb, kb, vb = strided_blocks(q), strided_blocks(k), strided_blocks(v)
    prev = lambda t: jnp.pad(t, ((0, 0), (0, 0), (1, 0), (0, 0), (0, 0), (0, 0)))[:, :, :-1]
    kw = jnp.concatenate([prev(kb), kb], axis=3)
    vw = jnp.concatenate([prev(vb), vb], axis=3)
    sc = jnp.einsum('brnqhd,brnkhd->brnhqk', qb, kw, preferred_element_type=jnp.float32) * (dh ** -0.5)
    a = jnp.arange(L)[:, None]
    c = jnp.arange(2 * L)[None, :]
    band = (c >= a) & (c <= a + L)
    has_prev = jnp.arange(nb) > 0
    valid = band[None] & ((c >= L)[None] | has_prev[:, None, None])
    sc = jnp.where(valid[None, None, :, None], sc, NEG)
    lse = jax.nn.logsumexp(sc, axis=-1)
    p = jnp.exp(sc - lse[..., None])
    o = jnp.einsum('brnhqk,brnkhd->brnqhd', p.astype(v.dtype), vw)
    o = o.reshape(b, dilation, n, h, dh).transpose(0, 2, 1, 3, 4).reshape(b, s_pad, h, dh)[:, :s]
    lse = lse.transpose(0, 1, 2, 4, 3).reshape(b, dilation, n, h).transpose(0, 2, 1, 3).reshape(b, s_pad, h)[:, :s]
    return o, lse


def hybrid_mixer(h, pos, w_in, g_q_lat, w_uq, g_kv_lat, w_ukv, w_branch, w_out):
    b, s, _ = h.shape
    proj = h @ w_in
    sb, mla, dil, gate_logits = jnp.split(
        proj, [SB_COLS, SB_COLS + MLA_COLS, SB_COLS + MLA_COLS + DIL_COLS], axis=-1)

    sb = sb.reshape(b, s, 3, SB_HEADS, HEAD_DIM)
    o_a = stick_breaking_attention(sb[:, :, 0], sb[:, :, 1], sb[:, :, 2])

    c_q, c_kv, k_rope_raw = jnp.split(mla, [MLA_Q_LORA, MLA_Q_LORA + MLA_KV_LORA], axis=-1)
    o_b = mla_attention(c_q, c_kv, k_rope_raw, pos, g_q_lat, w_uq, g_kv_lat, w_ukv)

    dil = dil.reshape(b, s, len(DIL_CONFIGS), 3, DIL_HEADS, HEAD_DIM)
    outs, lses = [], []
    for gi, (window, dilation) in enumerate(DIL_CONFIGS):
        qg = apply_rope(dil[:, :, gi, 0], pos)
        kg = apply_rope(dil[:, :, gi, 1], pos)
        o_g, lse_g = dilated_group_attention(qg, kg, dil[:, :, gi, 2], window, dilation)
        outs.append(o_g)
        lses.append(lse_g)
    alpha = jax.nn.softmax(jnp.stack(lses, axis=0), axis=0)
    o_c = jnp.sum(alpha[..., None] * jnp.stack(outs, axis=0).astype(jnp.float32), axis=0).astype(h.dtype)

    branches = jnp.stack([o_a.reshape(b, s, BRANCH_WIDTH),
                          o_b.reshape(b, s, BRANCH_WIDTH),
                          o_c.reshape(b, s, BRANCH_WIDTH)], axis=2)
    branch_d = jnp.einsum('bsnc,ncd->bsnd', branches, w_branch)
    gates = jax.nn.sigmoid(gate_logits.reshape(b, s, N_BRANCH, D_MODEL).astype(jnp.float32)).astype(h.dtype)
    merged = jnp.sum(gates * branch_d, axis=2)
    return merged @ w_out


def hierarchical_moe(h, w_group_router, b_group_router, w_expert_router, b_expert_router,
                     w_gate, w_up, w_down):
    b, s, d = h.shape
    t = h.reshape(b * s, d)
    group_logits = (t @ w_group_router).astype(jnp.float32) + b_group_router.astype(jnp.float32)
    group_prob = jax.nn.softmax(group_logits, axis=-1)
    g_top = jnp.argmax(group_logits, axis=-1)
    p_group = jnp.take_along_axis(group_prob, g_top[:, None], axis=-1)
    expert_logits = ((t @ w_expert_router).astype(jnp.float32)
                     + b_expert_router.astype(jnp.float32)).reshape(-1, N_GROUPS, EXPERTS_PER_GROUP)
    in_group = jnp.take_along_axis(expert_logits, g_top[:, None, None], axis=1)[:, 0]
    top_logit, top_i = lax.top_k(in_group, TOP_K)
    top_p = jax.nn.softmax(top_logit, axis=-1)
    expert_ids = g_top[:, None] * EXPERTS_PER_GROUP + top_i
    weights = p_group * top_p
    combine = jnp.sum(jax.nn.one_hot(expert_ids, N_EXPERTS, dtype=jnp.float32) * weights[..., None], axis=1)
    hid = jax.nn.silu(jnp.einsum('td,edf->tef', t, w_gate)) * jnp.einsum('td,edf->tef', t, w_up)
    out = jnp.einsum('tef,efd->td', hid * combine[:, :, None].astype(hid.dtype), w_down)
    return out.reshape(b, s, d)


def setup_inputs(seed: int = 0) -> dict:
    key = jax.random.key(seed)
    ks = jax.random.split(key, 24)
    f32 = jnp.float32

    def nrm(k, shape, fan_in):
        return jax.random.normal(k, shape, f32) * (fan_in ** -0.5)

    def gain(k, shape):
        return 1.0 + 0.02 * jax.random.normal(k, shape, f32)

    x = jax.random.normal(ks[0], (BATCH, SEQ, D_MODEL), f32)
    offset = jax.random.randint(ks[1], (BATCH, 1), 0, 4096, dtype=jnp.int32)
    positions = offset + jnp.arange(SEQ, dtype=jnp.int32)[None, :]
    return {
        "x": x,
        "positions": positions,
        "g_mix": gain(ks[2], (DEPTH, D_MODEL)),
        "w_in": nrm(ks[3], (DEPTH, D_MODEL, IN_COLS), D_MODEL),
        "g_q_lat": gain(ks[4], (DEPTH, MLA_Q_LORA)),
        "w_uq": nrm(ks[5], (DEPTH, MLA_Q_LORA, MLA_HEADS * (MLA_NOPE + MLA_ROPE)), MLA_Q_LORA),
        "g_kv_lat": gain(ks[6], (DEPTH, MLA_KV_LORA)),
        "w_ukv": nrm(ks[7], (DEPTH, MLA_KV_LORA, MLA_HEADS * (MLA_NOPE + MLA_V)), MLA_KV_LORA),
        "w_branch": nrm(ks[8], (DEPTH, N_BRANCH, BRANCH_WIDTH, D_MODEL), BRANCH_WIDTH),
        "w_out": nrm(ks[9], (DEPTH, D_MODEL, D_MODEL), D_MODEL),
        "g_ffn": gain(ks[10], (DEPTH, D_MODEL)),
        "w_group_router": nrm(ks[11], (DEPTH, D_MODEL, N_GROUPS), D_MODEL),
        "b_group_router": 0.01 * jax.random.normal(ks[12], (DEPTH, N_GROUPS), f32),
        "w_expert_router": nrm(ks[13], (DEPTH, D_MODEL, N_EXPERTS), D_MODEL),
        "b_expert_router": 0.01 * jax.random.normal(ks[14], (DEPTH, N_EXPERTS), f32),
        "w_gate": nrm(ks[15], (DEPTH, N_EXPERTS, D_MODEL, D_EXPERT), D_MODEL),
        "w_up": nrm(ks[16], (DEPTH, N_EXPERTS, D_MODEL, D_EXPERT), D_MODEL),
        "w_down": nrm(ks[17], (DEPTH, N_EXPERTS, D_EXPERT, D_MODEL), D_EXPERT),
        "g_final": gain(ks[18], (D_MODEL,)),
    }


def reference(x, positions, g_mix, w_in, g_q_lat, w_uq, g_kv_lat, w_ukv, w_branch, w_out, g_ffn,
              w_group_router, b_group_router, w_expert_router, b_expert_router,
              w_gate, w_up, w_down, g_final):
    for l in range(DEPTH):
        h = rmsnorm(x, g_mix[l])
        x = x + hybrid_mixer(h, positions, w_in[l], g_q_lat[l], w_uq[l], g_kv_lat[l], w_ukv[l],
                             w_branch[l], w_out[l])
        h = rmsnorm(x, g_ffn[l])
        x = x + hierarchical_moe(h, w_group_router[l], b_group_router[l], w_expert_router[l],
                                 b_expert_router[l], w_gate[l], w_up[l], w_down[l])
    return rmsnorm(x, g_final)
```

```python
import functools
import math

import jax
import jax.numpy as jnp
from jax import lax
from jax.experimental import pallas as pl
from jax.experimental.pallas import tpu as pltpu

F32 = jnp.float32
BF16 = jnp.bfloat16

D_MODEL = 1024
HEAD_DIM = 64
N_HEADS = 4
BRANCH_WIDTH = N_HEADS * HEAD_DIM
ROPE_THETA = 10000.0
EPS = 1e-6
NEG = -1e30
MLA_NOPE = 64
MLA_ROPE = 32
MLA_Q_LORA = 256
MLA_KV_LORA = 128
DIL_CONFIGS = ((128, 1), (512, 4), (2048, 16))
DIL_L = 128
N_DIL = len(DIL_CONFIGS)
N_BRANCH = 3
N_GROUPS = 4
EXPERTS_PER_GROUP = 4
D_EXPERT = 256
SB_COLS = 3 * BRANCH_WIDTH
MLA_COLS = MLA_Q_LORA + MLA_KV_LORA + MLA_ROPE
DIL_COLS = N_DIL * 3 * BRANCH_WIDTH
GATE_COLS = N_BRANCH * D_MODEL

LANES = 128
VMEM_LIMIT = 56 * 1024 * 1024

SB_STOP = -110.0

_NT = (((1,), (1,)), ((), ()))


def _cparams(*sem):
    return pltpu.CompilerParams(dimension_semantics=sem, vmem_limit_bytes=VMEM_LIMIT)


def _head_of_lane(width, per_head):
    return lax.broadcasted_iota(jnp.int32, (1, width), 1) // per_head


def _rope_table_kernel(pos_ref, dc_ref, dsa_ref, dsb_ref, mc_ref, msa_ref, msb_ref):
    pos = pos_ref[...]
    lane = lax.broadcasted_iota(jnp.int32, (1, LANES), 1)

    def tables(dim, in_rope, first_half, freq_idx):
        inv_freq = jnp.exp(freq_idx.astype(F32) * (-2.0 / dim * math.log(ROPE_THETA)))
        ang = pos * inv_freq
        cos, sin = jnp.cos(ang), jnp.sin(ang)
        c = jnp.where(in_rope, cos, 1.0)
        sa = jnp.where(in_rope & first_half, -sin, 0.0)
        sb = jnp.where(in_rope & jnp.logical_not(first_half), sin, 0.0)
        return c, sa, sb

    half = HEAD_DIM // 2
    c, sa, sb = tables(HEAD_DIM, lane >= 0, (lane % HEAD_DIM) < half, lane % half)
    dc_ref[...], dsa_ref[...], dsb_ref[...] = c, sa, sb
    half = MLA_ROPE // 2
    rel = lane - MLA_NOPE
    in_rope = (rel >= 0) & (rel < MLA_ROPE)
    c, sa, sb = tables(MLA_ROPE, in_rope, rel < half, jnp.where(in_rope, rel % half, 0))
    mc_ref[...], msa_ref[...], msb_ref[...] = c, sa, sb


def _rope_tables(pos_col, tm=1024):
    t = pos_col.shape[0]
    spec = pl.BlockSpec((tm, LANES), lambda i: (i, 0))
    return pl.pallas_call(
        _rope_table_kernel,
        grid=(t // tm,),
        in_specs=[pl.BlockSpec((tm, 1), lambda i: (i, 0))],
        out_specs=[spec] * 6,
        out_shape=[jax.ShapeDtypeStruct((t, LANES), F32)] * 6,
        compiler_params=_cparams("parallel"),
        name="rope_tables",
    )(pos_col)


def _rope_block(x, c, sa, sb, half):
    return x * c + pltpu.roll(x, LANES - half, 1) * sa + pltpu.roll(x, half, 1) * sb


def _rms(x, g):
    return x * lax.rsqrt(jnp.mean(x * x, axis=-1, keepdims=True) + EPS) * g


def _proj_kernel(x_ref, gmix_ref, wsb_ref, wmla_ref, wdil_ref, wg_ref, gq_ref, wuq_ref, gkv_ref, wuk_ref,
                 wuv_ref, dc_ref, dsa_ref, dsb_ref, mc_ref, msa_ref, msb_ref,
                 sb_ref, m_ref, dil_ref, g_ref):
    h = _rms(x_ref[...], gmix_ref[...]).astype(BF16)

    def mm(a, w):
        return jnp.dot(a, w, preferred_element_type=F32)

    for c in range(0, SB_COLS, BRANCH_WIDTH):
        sb_ref[:, c:c + BRANCH_WIDTH] = mm(h, wsb_ref[:, c:c + BRANCH_WIDTH]).astype(BF16)

    lat = mm(h, wmla_ref[...])
    cq = lat[:, :MLA_Q_LORA]
    ckv = lat[:, MLA_Q_LORA:MLA_Q_LORA + MLA_KV_LORA]
    kr = lat[:, MLA_Q_LORA + MLA_KV_LORA:]
    mc, msa, msb = mc_ref[...], msa_ref[...], msb_ref[...]
    kr = _rope_block(kr, mc, msa, msb, MLA_ROPE // 2)
    q = mm(_rms(cq, gq_ref[...]).astype(BF16), wuq_ref[...])
    ckv_n = _rms(ckv, gkv_ref[...]).astype(BF16)
    kn = mm(ckv_n, wuk_ref[...])
    vv = mm(ckv_n, wuv_ref[...])
    q_scale = (MLA_NOPE + MLA_ROPE) ** -0.5
    for hd in range(N_HEADS):
        sl = slice(hd * LANES, (hd + 1) * LANES)
        qh = _rope_block(q[:, sl], mc, msa, msb, MLA_ROPE // 2) * q_scale
        m_ref[:, sl] = qh.astype(BF16)
        m_ref[:, N_HEADS * LANES + hd * LANES:N_HEADS * LANES + (hd + 1) * LANES] = (kn[:, sl] + kr).astype(BF16)
    m_ref[:, 2 * N_HEADS * LANES:] = vv.astype(BF16)

    dc, dsa, dsb = dc_ref[...], dsa_ref[...], dsb_ref[...]
    for blk in range(N_DIL * 3):
        c0 = blk * BRANCH_WIDTH
        r = mm(h, wdil_ref[:, c0:c0 + BRANCH_WIDTH])
        if blk % 3 == 2:
            dil_ref[:, c0:c0 + BRANCH_WIDTH] = r.astype(BF16)
        else:
            for p in range(BRANCH_WIDTH // LANES):
                sl = slice(p * LANES, (p + 1) * LANES)
                dil_ref[:, c0 + p * LANES:c0 + (p + 1) * LANES] = _rope_block(
                    r[:, sl], dc, dsa, dsb, HEAD_DIM // 2).astype(BF16)

    gc = 512
    for c in range(0, GATE_COLS, gc):
        z = mm(h, wg_ref[:, c:c + gc])
        g_ref[:, c:c + gc] = (1.0 / (1.0 + jnp.exp(-z))).astype(BF16)


def _proj(x, gmix, wsb, wmla, wdil, wg, gq, wuq, gkv, wuk, wuv, tabs, tm=256):
    t = x.shape[0]
    row = lambda w: pl.BlockSpec((tm, w), lambda i: (i, 0))
    full = lambda a: pl.BlockSpec(a.shape, lambda i: (0,) * a.ndim)
    weights = (gmix, wsb, wmla, wdil, wg, gq, wuq, gkv, wuk, wuv)
    m_cols = 3 * N_HEADS * LANES
    return pl.pallas_call(
        _proj_kernel,
        grid=(t // tm,),
        in_specs=[row(D_MODEL)] + [full(w) for w in weights] + [row(LANES)] * 6,
        out_specs=[row(SB_COLS), row(m_cols), row(DIL_COLS), row(GATE_COLS)],
        out_shape=[jax.ShapeDtypeStruct((t, SB_COLS), BF16), jax.ShapeDtypeStruct((t, m_cols), BF16),
                   jax.ShapeDtypeStruct((t, DIL_COLS), BF16), jax.ShapeDtypeStruct((t, GATE_COLS), BF16)],
        compiler_params=_cparams("parallel"),
        name="proj",
    )(x, *weights, *tabs)


def _sb_kernel(q_ref, k_ref, v_ref, o_ref, acc_ref, c_ref, *, tq):
    i = pl.program_id(1)
    q = q_ref[...]
    head = _head_of_lane(BRANCH_WIDTH, HEAD_DIM)
    qm = [jnp.where(head == hd, q, jnp.zeros_like(q)) for hd in range(N_HEADS)]
    row = lax.broadcasted_iota(jnp.int32, (tq, tq), 0)
    col = lax.broadcasted_iota(jnp.int32, (tq, tq), 1)
    later = jnp.where(row > col, 1.0, 0.0).astype(BF16)
    ones = jnp.ones((tq, LANES), BF16)
    acc_ref[...] = jnp.zeros_like(acc_ref)
    c_ref[...] = jnp.zeros_like(c_ref)

    def body(carry):
        j, _ = carry
        k = k_ref[pl.ds(pl.multiple_of(j * tq, tq), tq), :]
        v = v_ref[pl.ds(pl.multiple_of(j * tq, tq), tq), :]
        strict = (col + j * tq) < (row + i * tq)
        cmax = jnp.float32(-jnp.inf)
        for hd in range(N_HEADS):
            z = lax.dot_general(qm[hd], k, _NT, preferred_element_type=F32)
            sp = jnp.log(1.0 + jnp.exp(-jnp.abs(z)))
            log_beta = jnp.minimum(z, 0.0) - sp
            log_1m = jnp.where(strict, log_beta - z, 0.0).astype(BF16)
            within = jnp.dot(log_1m, later, preferred_element_type=F32)
            total = jnp.dot(log_1m, ones, preferred_element_type=F32)
            c = c_ref[hd]
            c_wide = jnp.concatenate([c] * (tq // LANES), axis=1)
            w = jnp.where(strict, jnp.exp(log_beta + c_wide + within), 0.0)
            acc_ref[hd] += jnp.dot(w.astype(BF16), v, preferred_element_type=F32)
            c_new = c + total
            c_ref[hd] = c_new
            cmax = jnp.maximum(cmax, jnp.max(c_new))
        return j - 1, cmax

    lax.while_loop(lambda cr: (cr[0] >= 0) & (cr[1] > SB_STOP), body, (i, jnp.float32(0.0)))
    out = jnp.zeros((tq, BRANCH_WIDTH), F32)
    for hd in range(N_HEADS):
        out = jnp.where(head == hd, acc_ref[hd], out)
    o_ref[...] = out.astype(BF16)


def _sb_attention(sb, tq=256):
    b, s, _ = sb.shape
    return pl.pallas_call(
        functools.partial(_sb_kernel, tq=tq),
        grid=(b, s // tq),
        in_specs=[pl.BlockSpec((None, tq, BRANCH_WIDTH), lambda bi, i: (bi, i, 0)),
                  pl.BlockSpec((None, s, BRANCH_WIDTH), lambda bi, i: (bi, 0, 1)),
                  pl.BlockSpec((None, s, BRANCH_WIDTH), lambda bi, i: (bi, 0, 2))],
        out_specs=pl.BlockSpec((None, tq, BRANCH_WIDTH), lambda bi, i: (bi, i, 0)),
        out_shape=jax.ShapeDtypeStruct((b, s, BRANCH_WIDTH), BF16),
        scratch_shapes=[pltpu.VMEM((N_HEADS, tq, BRANCH_WIDTH), F32), pltpu.VMEM((N_HEADS, tq, LANES), F32)],
        compiler_params=_cparams("parallel", "parallel"),
        name="sb_attention",
    )(sb, sb, sb)


def _mla_kernel(q_ref, k_ref, v_ref, o_ref, *, tq):
    i = pl.program_id(2)
    q = q_ref[...]
    row = lax.broadcasted_iota(jnp.int32, (tq, tq), 0)
    col = lax.broadcasted_iota(jnp.int32, (tq, tq), 1)

    def step(j, carry, masked):
        m, l, acc = carry
        k = k_ref[pl.ds(pl.multiple_of(j * tq, tq), tq), :]
        v = v_ref[pl.ds(pl.multiple_of(j * tq, tq), tq), :]
        s = lax.dot_general(q, k, _NT, preferred_element_type=F32)
        if masked:
            s = jnp.where(col <= row, s, NEG)
        m_new = jnp.maximum(m, jnp.max(s, axis=1, keepdims=True))
        alpha = jnp.exp(m - m_new)
        p = jnp.exp(s - m_new)
        l = alpha * l + jnp.sum(p, axis=1, keepdims=True)
        acc = alpha * acc + jnp.dot(p.astype(BF16), v, preferred_element_type=F32)
        return m_new, l, acc

    init = (jnp.full((tq, 1), NEG, F32), jnp.zeros((tq, 1), F32), jnp.zeros((tq, LANES), F32))
    carry = lax.fori_loop(0, i, lambda j, cr: step(j, cr, False), init)
    _, l, acc = step(i, carry, True)
    o_ref[...] = (acc / l).astype(BF16)


def _mla_attention(m, tq=256):
    b, s, _ = m.shape
    return pl.pallas_call(
        functools.partial(_mla_kernel, tq=tq),
        grid=(b, N_HEADS, s // tq),
        in_specs=[pl.BlockSpec((None, tq, LANES), lambda bi, hd, i: (bi, i, hd)),
                  pl.BlockSpec((None, s, LANES), lambda bi, hd, i: (bi, 0, N_HEADS + hd)),
                  pl.BlockSpec((None, s, LANES), lambda bi, hd, i: (bi, 0, 2 * N_HEADS + hd))],
        out_specs=pl.BlockSpec((None, tq, LANES), lambda bi, hd, i: (bi, i, hd)),
        out_shape=jax.ShapeDtypeStruct((b, s, N_HEADS * LANES), BF16),
        compiler_params=_cparams("parallel", "parallel", "parallel"),
        name="mla_attention",
    )(m, m, m)


def _dil_kernel(q_ref, kc_ref, kp_ref, vc_ref, vp_ref, o_ref, lse_ref, *, rows):
    has_prev_chunk = pl.program_id(2) > 0
    head = _head_of_lane(BRANCH_WIDTH, HEAD_DIM)
    L = DIL_L
    a = lax.broadcasted_iota(jnp.int32, (L, 2 * L), 0)
    c = lax.broadcasted_iota(jnp.int32, (L, 2 * L), 1)
    band = (c >= a) & (c <= a + L)
    for sub in range(rows // L):
        q = q_ref[sub * L:(sub + 1) * L, :]
        if sub == 0:
            kw = jnp.concatenate([kp_ref[...], kc_ref[:L, :]], axis=0)
            vw = jnp.concatenate([vp_ref[...], vc_ref[:L, :]], axis=0)
            valid = band & ((c >= L) | has_prev_chunk)
        else:
            kw = kc_ref[(sub - 1) * L:(sub + 1) * L, :]
            vw = vc_ref[(sub - 1) * L:(sub + 1) * L, :]
            valid = band
        o = jnp.zeros((L, BRANCH_WIDTH), F32)
        lse_all = jnp.zeros((L, BRANCH_WIDTH), F32)
        for hd in range(N_HEADS):
            qm = jnp.where(head == hd, q, jnp.zeros_like(q))
            sc = lax.dot_general(qm, kw, _NT, preferred_element_type=F32)
            sc = jnp.where(valid, sc, NEG)
            mx = jnp.max(sc, axis=1, keepdims=True)
            p = jnp.exp(sc - mx)
            sm = jnp.sum(p, axis=1, keepdims=True)
            oh = jnp.dot((p / sm).astype(BF16), vw, preferred_element_type=F32)
            o = jnp.where(head == hd, oh, o)
            lse_all = jnp.where(head == hd, mx + jnp.log(sm), lse_all)
        o_ref[sub * L:(sub + 1) * L, :] = o.astype(BF16)
        lse_ref[sub * L:(sub + 1) * L, :] = lse_all


def _dil_attention(dil, gi):
    b, s, _ = dil.shape
    window, d = DIL_CONFIGS[gi]
    assert window // d == DIL_L and s % window == 0
    n = s // d
    rows = min(512, n)
    view = dil.reshape(b, n, d * DIL_COLS)
    cb = DIL_COLS // BRANCH_WIDTH
    per_chunk = rows // DIL_L

    def cur(t):
        return pl.BlockSpec((None, rows, BRANCH_WIDTH), lambda bi, r, ch: (bi, ch, r * cb + 3 * gi + t))

    def prev(t):
        return pl.BlockSpec((None, DIL_L, BRANCH_WIDTH),
                            lambda bi, r, ch: (bi, jnp.maximum(ch * per_chunk - 1, 0), r * cb + 3 * gi + t))

    out_spec = pl.BlockSpec((None, rows, BRANCH_WIDTH), lambda bi, r, ch: (bi, ch, r))
    o, lse = pl.pallas_call(
        functools.partial(_dil_kernel, rows=rows),
        grid=(b, d, n // rows),
        in_specs=[cur(0), cur(1), prev(1), cur(2), prev(2)],
        out_specs=[out_spec, out_spec],
        out_shape=[jax.ShapeDtypeStruct((b, n, d * BRANCH_WIDTH), BF16),
                   jax.ShapeDtypeStruct((b, n, d * BRANCH_WIDTH), F32)],
        compiler_params=_cparams("parallel", "parallel", "parallel"),
        name=f"dil_attention_{gi}",
    )(view, view, view, view, view)
    return o.reshape(b * s, BRANCH_WIDTH), lse.reshape(b * s, BRANCH_WIDTH)


def _merge_kernel(x_ref, oa_ref, ob_ref, o0_ref, o1_ref, o2_ref, l0_ref, l1_ref, l2_ref, g_ref,
                  wa_ref, wb_ref, wc_ref, wo_ref, out_ref):
    l0, l1, l2 = l0_ref[...], l1_ref[...], l2_ref[...]
    mx = jnp.maximum(jnp.maximum(l0, l1), l2)
    e0, e1, e2 = jnp.exp(l0 - mx), jnp.exp(l1 - mx), jnp.exp(l2 - mx)
    oc = (e0 * o0_ref[...].astype(F32) + e1 * o1_ref[...].astype(F32) + e2 * o2_ref[...].astype(F32)) / (e0 + e1 + e2)
    merged = g_ref[:, :D_MODEL].astype(F32) * jnp.dot(oa_ref[...], wa_ref[...], preferred_element_type=F32)
    merged += g_ref[:, D_MODEL:2 * D_MODEL].astype(F32) * jnp.dot(ob_ref[...], wb_ref[...], preferred_element_type=F32)
    merged += g_ref[:, 2 * D_MODEL:].astype(F32) * jnp.dot(oc.astype(BF16), wc_ref[...], preferred_element_type=F32)
    out_ref[...] = x_ref[...] + jnp.dot(merged.astype(BF16), wo_ref[...], preferred_element_type=F32)


def _merge(x, oa, ob, dil_o, dil_lse, gates, wa, wb, wc, wo, tm=512):
    t = x.shape[0]
    row = lambda w: pl.BlockSpec((tm, w), lambda i: (i, 0))
    full = lambda a: pl.BlockSpec(a.shape, lambda i: (0,) * a.ndim)
    return pl.pallas_call(
        _merge_kernel,
        grid=(t // tm,),
        in_specs=[row(D_MODEL), row(BRANCH_WIDTH), row(N_HEADS * LANES)] + [row(BRANCH_WIDTH)] * 6
                 + [row(GATE_COLS), full(wa), full(wb), full(wc), full(wo)],
        out_specs=row(D_MODEL),
        out_shape=jax.ShapeDtypeStruct((t, D_MODEL), F32),
        compiler_params=_cparams("parallel"),
        name="merge",
    )(x, oa, ob, *dil_o, *dil_lse, gates, wa, wb, wc, wo)


def _route(logits):
    col = lambda k: logits[:, k:k + 1]
    gl = [col(k) for k in range(N_GROUPS)]
    gmax = functools.reduce(jnp.maximum, gl)
    taken = jnp.zeros_like(gmax) > 1.0
    is_g = []
    for k in range(N_GROUPS):
        hit = jnp.logical_and(gl[k] == gmax, jnp.logical_not(taken))
        is_g.append(hit)
        taken = jnp.logical_or(taken, hit)
    p_group = 1.0 / functools.reduce(lambda u, w: u + w, [jnp.exp(g - gmax) for g in gl])
    a = []
    for e in range(EXPERTS_PER_GROUP):
        v = col(N_GROUPS + (N_GROUPS - 1) * EXPERTS_PER_GROUP + e)
        for k in range(N_GROUPS - 2, -1, -1):
            v = jnp.where(is_g[k], col(N_GROUPS + k * EXPERTS_PER_GROUP + e), v)
        a.append(v)
    t1 = functools.reduce(jnp.maximum, a)
    taken = jnp.zeros_like(t1) > 1.0
    first = []
    for e in range(EXPERTS_PER_GROUP):
        hit = jnp.logical_and(a[e] == t1, jnp.logical_not(taken))
        first.append(hit)
        taken = jnp.logical_or(taken, hit)
    rest = [jnp.where(first[e], -jnp.inf, a[e]) for e in range(EXPERTS_PER_GROUP)]
    t2 = functools.reduce(jnp.maximum, rest)
    taken = jnp.zeros_like(t1) > 1.0
    second = []
    for e in range(EXPERTS_PER_GROUP):
        hit = jnp.logical_and(jnp.logical_and(rest[e] == t2, jnp.logical_not(first[e])), jnp.logical_not(taken))
        second.append(hit)
        taken = jnp.logical_or(taken, hit)
    d = jnp.exp(t2 - t1)
    p1 = 1.0 / (1.0 + d)
    p2 = d / (1.0 + d)
    lane = lax.broadcasted_iota(jnp.int32, (1, LANES), 1)
    w_lane = jnp.zeros(logits.shape, F32)
    for e in range(EXPERTS_PER_GROUP):
        w_e = p_group * jnp.where(first[e], p1, jnp.where(second[e], p2, 0.0))
        w_lane = jnp.where(lane == e, w_e, w_lane)
    return [jnp.where(is_g[k], w_lane, 0.0) for k in range(N_GROUPS)]


def _moe_kernel(x_ref, g_ref, wr_hi_ref, wr_lo_ref, br_ref, wg_ref, wu_ref, wd_ref, gfin_ref, out_ref,
                h_ref, comb_ref, acc_ref, *, final):
    gi = pl.program_id(1)

    @pl.when(gi == 0)
    def _():
        x = x_ref[...]
        hf = _rms(x, g_ref[...])
        h_hi = hf.astype(BF16)
        h_lo = (hf - h_hi.astype(F32)).astype(BF16)
        logits = (jnp.dot(h_hi, wr_hi_ref[...], preferred_element_type=F32)
                  + jnp.dot(h_lo, wr_hi_ref[...], preferred_element_type=F32)
                  + jnp.dot(h_hi, wr_lo_ref[...], preferred_element_type=F32)) + br_ref[...]
        comb = _route(logits)
        for k in range(N_GROUPS):
            comb_ref[k] = comb[k]
        h_ref[...] = h_hi
        acc_ref[...] = x

    h = h_ref[...]
    gate = jnp.dot(h, wg_ref[...], preferred_element_type=F32)
    up = jnp.dot(h, wu_ref[...], preferred_element_type=F32)
    comb = comb_ref[gi]
    hid = gate * (1.0 / (1.0 + jnp.exp(-gate))) * up
    parts = [hid[:, e * D_EXPERT:(e + 1) * D_EXPERT] * comb[:, e:e + 1] for e in range(EXPERTS_PER_GROUP)]
    hid = jnp.concatenate(parts, axis=1).astype(BF16)
    acc_ref[...] += jnp.dot(hid, wd_ref[...], preferred_element_type=F32)

    @pl.when(gi == N_GROUPS - 1)
    def _():
        y = acc_ref[...]
        out_ref[...] = _rms(y, gfin_ref[...]) if final else y


def _moe(x, g_ffn, wr_hi, wr_lo, br, wg, wu, wd, g_final, final, tm=512):
    t = x.shape[0]
    row = pl.BlockSpec((tm, D_MODEL), lambda i, k: (i, 0))
    full = lambda a: pl.BlockSpec(a.shape, lambda i, k: (0,) * a.ndim)
    grp = pl.BlockSpec((None, D_MODEL, D_MODEL), lambda i, k: (k, 0, 0))
    return pl.pallas_call(
        functools.partial(_moe_kernel, final=final),
        grid=(t // tm, N_GROUPS),
        in_specs=[row, full(g_ffn), full(wr_hi), full(wr_lo), full(br), grp, grp, grp, full(g_final)],
        out_specs=row,
        out_shape=jax.ShapeDtypeStruct((t, D_MODEL), F32),
        scratch_shapes=[pltpu.VMEM((tm, D_MODEL), BF16), pltpu.VMEM((N_GROUPS, tm, LANES), F32),
                        pltpu.VMEM((tm, D_MODEL), F32)],
        compiler_params=_cparams("parallel", "arbitrary"),
        name="moe",
    )(x, g_ffn, wr_hi, wr_lo, br, wg, wu, wd, g_final)


def _pad_heads(w, per_head, keep):
    lead = w.shape[:-1]
    w = w.reshape(*lead, N_HEADS, per_head)[..., keep]
    w = jnp.pad(w, [(0, 0)] * (w.ndim - 1) + [(0, LANES - w.shape[-1])])
    return w.reshape(*lead, N_HEADS * LANES)


def _prepare_weights(w_in, w_uq, w_ukv, w_branch, w_out, w_group_router, b_group_router, w_expert_router,
                     b_expert_router, w_gate, w_up, w_down):
    depth = w_in.shape[0]
    q_scale = HEAD_DIM ** -0.5
    sb_scale = jnp.concatenate([jnp.full((BRANCH_WIDTH,), q_scale, F32), jnp.ones((2 * BRANCH_WIDTH,), F32)])
    o = 0
    wsb = (w_in[:, :, o:o + SB_COLS] * sb_scale).astype(BF16)
    o += SB_COLS
    w_lat = w_in[:, :, o:o + MLA_Q_LORA + MLA_KV_LORA]
    o += MLA_Q_LORA + MLA_KV_LORA
    w_kr = w_in[:, :, o:o + MLA_ROPE]
    o += MLA_ROPE
    w_kr = jnp.pad(w_kr, ((0, 0), (0, 0), (MLA_NOPE, LANES - MLA_NOPE - MLA_ROPE)))
    wmla = jnp.concatenate([w_lat, w_kr], axis=2).astype(BF16)
    dil_scale = jnp.tile(sb_scale, N_DIL)
    wdil = (w_in[:, :, o:o + DIL_COLS] * dil_scale).astype(BF16)
    o += DIL_COLS
    wg = w_in[:, :, o:].astype(BF16)
    wuq = _pad_heads(w_uq, MLA_NOPE + MLA_ROPE, slice(None)).astype(BF16)
    wuk = _pad_heads(w_ukv, 2 * MLA_NOPE, slice(0, MLA_NOPE)).astype(BF16)
    wuv = _pad_heads(w_ukv, 2 * MLA_NOPE, slice(MLA_NOPE, None)).astype(BF16)
    wa = w_branch[:, 0].astype(BF16)
    wb = w_branch[:, 1].reshape(depth, N_HEADS, HEAD_DIM, D_MODEL)
    wb = jnp.pad(wb, ((0, 0), (0, 0), (0, LANES - HEAD_DIM), (0, 0))).reshape(depth, N_HEADS * LANES, D_MODEL)
    wb = wb.astype(BF16)
    wc = w_branch[:, 2].astype(BF16)
    wo = w_out.astype(BF16)
    n_exp = N_GROUPS * EXPERTS_PER_GROUP
    wr = jnp.concatenate([w_group_router, w_expert_router], axis=2)
    wr = jnp.pad(wr, ((0, 0), (0, 0), (0, LANES - N_GROUPS - n_exp)))
    wr_hi = wr.astype(BF16)
    wr_lo = (wr - wr_hi.astype(F32)).astype(BF16)
    br = jnp.concatenate([b_group_router, b_expert_router], axis=1)
    br = jnp.pad(br, ((0, 0), (0, LANES - N_GROUPS - n_exp)))[:, None, :]

    def by_group(w):
        w = w.reshape(depth, N_GROUPS, EXPERTS_PER_GROUP, D_MODEL, D_EXPERT)
        return w.transpose(0, 1, 3, 2, 4).reshape(depth, N_GROUPS, D_MODEL, EXPERTS_PER_GROUP * D_EXPERT).astype(BF16)

    wgate, wup = by_group(w_gate), by_group(w_up)
    wdown = w_down.reshape(depth, N_GROUPS, EXPERTS_PER_GROUP * D_EXPERT, D_MODEL).astype(BF16)
    return dict(wsb=wsb, wmla=wmla, wdil=wdil, wg=wg, wuq=wuq, wuk=wuk, wuv=wuv, wa=wa, wb=wb, wc=wc, wo=wo,
                wr_hi=wr_hi, wr_lo=wr_lo, br=br, wgate=wgate, wup=wup, wdown=wdown)


def kernel(x, positions, g_mix, w_in, g_q_lat, w_uq, g_kv_lat, w_ukv, w_branch, w_out, g_ffn, w_group_router, b_group_router, w_expert_router, b_expert_router, w_gate, w_up, w_down, g_final):
    b, s, d = x.shape
    depth = w_in.shape[0]
    assert d == D_MODEL and w_in.shape[2] == SB_COLS + MLA_COLS + DIL_COLS + GATE_COLS
    t = b * s
    w = _prepare_weights(w_in, w_uq, w_ukv, w_branch, w_out, w_group_router, b_group_router, w_expert_router,
                         b_expert_router, w_gate, w_up, w_down)
    tabs = _rope_tables(positions.reshape(t, 1).astype(F32))
    xf = x.reshape(t, d)
    g_fin = g_final.reshape(1, d)
    for l in range(depth):
        sb, m, dil, gates = _proj(xf, g_mix[l][None], w["wsb"][l], w["wmla"][l], w["wdil"][l], w["wg"][l],
                                  g_q_lat[l][None], w["wuq"][l], g_kv_lat[l][None], w["wuk"][l], w["wuv"][l], tabs)
        oa = _sb_attention(sb.reshape(b, s, -1)).reshape(t, -1)
        ob = _mla_attention(m.reshape(b, s, -1)).reshape(t, -1)
        dil3 = dil.reshape(b, s, -1)
        dil_o, dil_lse = zip(*[_dil_attention(dil3, gi) for gi in range(N_DIL)])
        xf = _merge(xf, oa, ob, dil_o, dil_lse, gates, w["wa"][l], w["wb"][l], w["wc"][l], w["wo"][l])
        xf = _moe(xf, g_ffn[l][None], w["wr_hi"][l], w["wr_lo"][l], w["br"][l], w["wgate"][l], w["wup"][l],
                  w["wdown"][l], g_fin, final=(l == depth - 1))
    return xf.reshape(b, s, d)
```

```python
import functools
import math

import jax
import jax.numpy as jnp
from jax import lax
from jax.experimental import pallas as pl
from jax.experimental.pallas import tpu as pltpu

F32 = jnp.float32
BF16 = jnp.bfloat16

D_MODEL = 1024
HEAD_DIM = 64
N_HEADS = 4
BRANCH_WIDTH = N_HEADS * HEAD_DIM
ROPE_THETA = 10000.0
EPS = 1e-6
NEG = -1e30
MLA_NOPE = 64
MLA_ROPE = 32
MLA_Q_LORA = 256
MLA_KV_LORA = 128
DIL_CONFIGS = ((128, 1), (512, 4), (2048, 16))
DIL_L = 128
N_DIL = len(DIL_CONFIGS)
N_BRANCH = 3
N_GROUPS = 4
EXPERTS_PER_GROUP = 4
D_EXPERT = 256
SB_COLS = 3 * BRANCH_WIDTH
MLA_COLS = MLA_Q_LORA + MLA_KV_LORA + MLA_ROPE
DIL_COLS = N_DIL * 3 * BRANCH_WIDTH
GATE_COLS = N_BRANCH * D_MODEL
MLA_V = 64
LOG2_E = math.log2(math.e)

LANES = 128
VMEM_LIMIT = 56 * 1024 * 1024
DIL_SLABS = DIL_COLS // LANES
DIL_CHUNK = DIL_CONFIGS[-1][0]

SB_STOP = -110.0

_NT = (((1,), (1,)), ((), ()))


def _cparams(*sem):
    return pltpu.CompilerParams(dimension_semantics=sem, vmem_limit_bytes=VMEM_LIMIT)


def _head_of_lane(width, per_head):
    return lax.broadcasted_iota(jnp.int32, (1, width), 1) // per_head


def _rope_table_kernel(pos_ref, dc_ref, dsa_ref, dsb_ref, mc_ref, msa_ref, msb_ref):
    pos = pos_ref[...]
    lane = lax.broadcasted_iota(jnp.int32, (1, LANES), 1)

    def tables(dim, in_rope, first_half, freq_idx):
        inv_freq = jnp.exp(freq_idx.astype(F32) * (-2.0 / dim * math.log(ROPE_THETA)))
        ang = pos * inv_freq
        cos, sin = jnp.cos(ang), jnp.sin(ang)
        c = jnp.where(in_rope, cos, 1.0)
        sa = jnp.where(in_rope & first_half, -sin, 0.0)
        sb = jnp.where(in_rope & jnp.logical_not(first_half), sin, 0.0)
        return c, sa, sb

    half = HEAD_DIM // 2
    c, sa, sb = tables(HEAD_DIM, lane >= 0, (lane % HEAD_DIM) < half, lane % half)
    dc_ref[...], dsa_ref[...], dsb_ref[...] = c, sa, sb
    half = MLA_ROPE // 2
    rel = lane - MLA_NOPE
    in_rope = (rel >= 0) & (rel < MLA_ROPE)
    c, sa, sb = tables(MLA_ROPE, in_rope, rel < half, jnp.where(in_rope, rel % half, 0))
    mc_ref[...], msa_ref[...], msb_ref[...] = c, sa, sb


def _rope_tables(pos_col, tm=1024):
    t = pos_col.shape[0]
    spec = pl.BlockSpec((tm, LANES), lambda i: (i, 0))
    return pl.pallas_call(
        _rope_table_kernel,
        grid=(t // tm,),
        in_specs=[pl.BlockSpec((tm, 1), lambda i: (i, 0))],
        out_specs=[spec] * 6,
        out_shape=[jax.ShapeDtypeStruct((t, LANES), F32)] * 6,
        compiler_params=_cparams("parallel"),
        name="rope_tables",
    )(pos_col)


def _rope_block(x, c, sa, sb, half):
    return x * c + pltpu.roll(x, LANES - half, 1) * sa + pltpu.roll(x, half, 1) * sb


def _rms(x, g):
    return x * lax.rsqrt(jnp.mean(x * x, axis=-1, keepdims=True) + EPS) * g


def _proj_kernel(x_ref, gmix_ref, wsb_ref, wmla_ref, wdil_ref, wg_ref, gq_ref, wuq_ref, gkv_ref, wuk_ref,
                 wuv_ref, dc_ref, dsa_ref, dsb_ref, mc_ref, msa_ref, msb_ref,
                 sb_ref, m_ref, dil_ref, g_ref):
    h = _rms(x_ref[...], gmix_ref[...]).astype(BF16)

    def mm(a, w):
        return jnp.dot(a, w, preferred_element_type=F32)

    for c in range(0, SB_COLS, BRANCH_WIDTH):
        sb_ref[:, c:c + BRANCH_WIDTH] = mm(h, wsb_ref[:, c:c + BRANCH_WIDTH]).astype(BF16)

    lat = mm(h, wmla_ref[...])
    cq = lat[:, :MLA_Q_LORA]
    ckv = lat[:, MLA_Q_LORA:MLA_Q_LORA + MLA_KV_LORA]
    kr = lat[:, MLA_Q_LORA + MLA_KV_LORA:]
    mc, msa, msb = mc_ref[...], msa_ref[...], msb_ref[...]
    kr = _rope_block(kr, mc, msa, msb, MLA_ROPE // 2)
    q = mm(_rms(cq, gq_ref[...]).astype(BF16), wuq_ref[...])
    ckv_n = _rms(ckv, gkv_ref[...]).astype(BF16)
    kn = mm(ckv_n, wuk_ref[...])
    vv = mm(ckv_n, wuv_ref[...])
    q_scale = (MLA_NOPE + MLA_ROPE) ** -0.5 * LOG2_E
    for hd in range(N_HEADS):
        sl = slice(hd * LANES, (hd + 1) * LANES)
        qh = _rope_block(q[:, sl], mc, msa, msb, MLA_ROPE // 2) * q_scale
        m_ref[:, sl] = qh.astype(BF16)
        m_ref[:, N_HEADS * LANES + hd * LANES:N_HEADS * LANES + (hd + 1) * LANES] = (kn[:, sl] + kr).astype(BF16)
    lane = lax.broadcasted_iota(jnp.int32, (1, N_HEADS * LANES), 1)
    m_ref[:, 2 * N_HEADS * LANES:] = jnp.where(lane % LANES == MLA_V, 1.0, vv).astype(BF16)

    dc, dsa, dsb = dc_ref[...], dsa_ref[...], dsb_ref[...]
    for blk in range(N_DIL * 3):
        c0 = blk * BRANCH_WIDTH
        r = mm(h, wdil_ref[:, c0:c0 + BRANCH_WIDTH])
        for p in range(BRANCH_WIDTH // LANES):
            rp = r[:, p * LANES:(p + 1) * LANES]
            if blk % 3 != 2:
                rp = _rope_block(rp, dc, dsa, dsb, HEAD_DIM // 2)
            dil_ref[blk * (BRANCH_WIDTH // LANES) + p] = rp

    gc = 512
    for c in range(0, GATE_COLS, gc):
        z = mm(h, wg_ref[:, c:c + gc])
        g_ref[:, c:c + gc] = (1.0 / (1.0 + jnp.exp(-z))).astype(BF16)


def _proj(x, batch, gmix, wsb, wmla, wdil, wg, gq, wuq, gkv, wuk, wuv, tabs, tm=256):
    t = x.shape[0]
    per_batch = t // batch // tm
    row = lambda w: pl.BlockSpec((tm, w), lambda i: (i, 0))
    full = lambda a: pl.BlockSpec(a.shape, lambda i: (0,) * a.ndim)
    weights = (gmix, wsb, wmla, wdil, wg, gq, wuq, gkv, wuk, wuv)
    m_cols = 3 * N_HEADS * LANES
    dil_spec = pl.BlockSpec((None, DIL_SLABS, tm, LANES), lambda i: (i // per_batch, 0, i % per_batch, 0))
    return pl.pallas_call(
        _proj_kernel,
        grid=(t // tm,),
        in_specs=[row(D_MODEL)] + [full(w) for w in weights] + [row(LANES)] * 6,
        out_specs=[row(SB_COLS), row(m_cols), dil_spec, row(GATE_COLS)],
        out_shape=[jax.ShapeDtypeStruct((t, SB_COLS), BF16), jax.ShapeDtypeStruct((t, m_cols), BF16),
                   jax.ShapeDtypeStruct((batch, DIL_SLABS, t // batch, LANES), F32),
                   jax.ShapeDtypeStruct((t, GATE_COLS), BF16)],
        compiler_params=_cparams("parallel"),
        name="proj",
    )(x, *weights, *tabs)


def _sb_kernel(q_ref, k_ref, v_ref, o_ref, acc_ref, c_ref, *, tq):
    i = pl.program_id(1)
    q = q_ref[...]
    head = _head_of_lane(BRANCH_WIDTH, HEAD_DIM)
    qm = [jnp.where(head == hd, q, jnp.zeros_like(q)) for hd in range(N_HEADS)]
    row = lax.broadcasted_iota(jnp.int32, (tq, tq), 0)
    col = lax.broadcasted_iota(jnp.int32, (tq, tq), 1)
    later = jnp.where(row > col, 1.0, 0.0).astype(BF16)
    ones = jnp.ones((tq, LANES), BF16)
    acc_ref[...] = jnp.zeros_like(acc_ref)
    c_ref[...] = jnp.zeros_like(c_ref)

    def block(j, diagonal):
        k = k_ref[pl.ds(pl.multiple_of(j * tq, tq), tq), :]
        v = v_ref[pl.ds(pl.multiple_of(j * tq, tq), tq), :]
        cmax = jnp.float32(-jnp.inf)
        for hd in range(N_HEADS):
            z = lax.dot_general(qm[hd], k, _NT, preferred_element_type=F32)
            sp = jnp.log(1.0 + jnp.exp(-jnp.abs(z)))
            log_beta = jnp.minimum(z, 0.0) - sp
            log_1m = log_beta - z
            if diagonal:
                log_1m = jnp.where(col < row, log_1m, 0.0)
            log_1m = log_1m.astype(BF16)
            within = jnp.dot(log_1m, later, preferred_element_type=F32)
            total = jnp.dot(log_1m, ones, preferred_element_type=F32)
            c = c_ref[hd]
            c_wide = jnp.concatenate([c] * (tq // LANES), axis=1)
            w = jnp.exp(log_beta + c_wide + within)
            if diagonal:
                w = jnp.where(col < row, w, 0.0)
            acc_ref[hd] += jnp.dot(w.astype(BF16), v, preferred_element_type=F32)
            c_new = c + total
            c_ref[hd] = c_new
            cmax = jnp.maximum(cmax, jnp.max(c_new))
        return cmax

    first = block(i, True)
    lax.while_loop(lambda cr: (cr[0] >= 0) & (cr[1] > SB_STOP), lambda cr: (cr[0] - 1, block(cr[0], False)),
                   (i - 1, first))
    out = jnp.zeros((tq, BRANCH_WIDTH), F32)
    for hd in range(N_HEADS):
        out = jnp.where(head == hd, acc_ref[hd], out)
    o_ref[...] = out.astype(BF16)


def _sb_attention(sb, tq=256):
    b, s, _ = sb.shape
    return pl.pallas_call(
        functools.partial(_sb_kernel, tq=tq),
        grid=(b, s // tq),
        in_specs=[pl.BlockSpec((None, tq, BRANCH_WIDTH), lambda bi, i: (bi, i, 0)),
                  pl.BlockSpec((None, s, BRANCH_WIDTH), lambda bi, i: (bi, 0, 1)),
                  pl.BlockSpec((None, s, BRANCH_WIDTH), lambda bi, i: (bi, 0, 2))],
        out_specs=pl.BlockSpec((None, tq, BRANCH_WIDTH), lambda bi, i: (bi, i, 0)),
        out_shape=jax.ShapeDtypeStruct((b, s, BRANCH_WIDTH), BF16),
        scratch_shapes=[pltpu.VMEM((N_HEADS, tq, BRANCH_WIDTH), F32), pltpu.VMEM((N_HEADS, tq, LANES), F32)],
        compiler_params=_cparams("parallel", "parallel"),
        name="sb_attention",
    )(sb, sb, sb)


def _mla_kernel(q_ref, k_ref, v_ref, o_ref, m_ref, acc_ref, *, t):
    i = pl.program_id(1)
    row = lax.broadcasted_iota(jnp.int32, (t, t), 0)
    col = lax.broadcasted_iota(jnp.int32, (t, t), 1)
    m_ref[...] = jnp.full(m_ref.shape, NEG, F32)
    acc_ref[...] = jnp.zeros_like(acc_ref)

    def step(j, masked):
        rows = pl.ds(pl.multiple_of(j * t, t), t)
        for hd in range(N_HEADS):
            sl = slice(hd * LANES, (hd + 1) * LANES)
            s = lax.dot_general(q_ref[:, sl], k_ref[rows, sl], _NT, preferred_element_type=F32)
            if masked:
                s = jnp.where(col <= row, s, NEG)
            m_prev = m_ref[hd]
            m_new = jnp.maximum(m_prev, jnp.max(s, axis=1, keepdims=True))
            p = jnp.exp2(s - jnp.concatenate([m_new] * (t // LANES), axis=1)).astype(BF16)
            acc_ref[hd] = jnp.exp2(m_prev - m_new) * acc_ref[hd] + jnp.dot(p, v_ref[rows, sl],
                                                                           preferred_element_type=F32)
            m_ref[hd] = m_new

    def body(j, carry):
        step(j, False)
        return carry

    lax.fori_loop(0, i, body, 0)
    step(i, True)
    for hd in range(N_HEADS):
        acc = acc_ref[hd]
        o_ref[:, hd * LANES:(hd + 1) * LANES] = (acc / acc[:, MLA_V:MLA_V + 1]).astype(BF16)


def _mla_attention(m, t=512):
    b, s, _ = m.shape
    w = N_HEADS * LANES
    t = min(t, s)
    return pl.pallas_call(
        functools.partial(_mla_kernel, t=t),
        grid=(b, s // t),
        in_specs=[pl.BlockSpec((None, t, w), lambda bi, i: (bi, i, 0)),
                  pl.BlockSpec((None, s, w), lambda bi, i: (bi, 0, 1)),
                  pl.BlockSpec((None, s, w), lambda bi, i: (bi, 0, 2))],
        out_specs=pl.BlockSpec((None, t, w), lambda bi, i: (bi, i, 0)),
        out_shape=jax.ShapeDtypeStruct((b, s, w), BF16),
        scratch_shapes=[pltpu.VMEM((N_HEADS, t, LANES), F32), pltpu.VMEM((N_HEADS, t, LANES), F32)],
        compiler_params=_cparams("parallel", "parallel"),
        name="mla_attention",
    )(m, m, m)


def _small_unroll(n):
    return 2 if n % 2 == 0 else (3 if n % 3 == 0 else 1)


def _dil_kernel(cur_ref, kp_ref, vp_ref, out_ref, og_ref, lg_ref):
    g = pl.program_id(2)
    has_prev_chunk = pl.program_id(1) > 0
    L, C = DIL_L, DIL_CHUNK
    halves = BRANCH_WIDTH // LANES
    head = _head_of_lane(BRANCH_WIDTH, HEAD_DIM)
    a = lax.broadcasted_iota(jnp.int32, (N_HEADS * L, 2 * L), 0) % L
    c = lax.broadcasted_iota(jnp.int32, (N_HEADS * L, 2 * L), 1)
    band = (c >= a) & (c <= a + L)
    band_first = band & ((c >= L) | has_prev_chunk)

    def rows_of(start, d):
        return pl.ds(start, L) if d == 1 else pl.ds(start, L, stride=d)

    def load(ref, slab, start, d):
        idx = rows_of(start, d)
        return jnp.concatenate([ref[slab + p, idx, :] for p in range(halves)], axis=1).astype(BF16)

    def unit(gi, d, q_start, kprev, vprev, valid):
        q = load(cur_ref, 0, q_start, d)
        kw = jnp.concatenate([kprev, load(cur_ref, halves, q_start, d)], axis=0)
        vw = jnp.concatenate([vprev, load(cur_ref, 2 * halves, q_start, d)], axis=0)
        qs = jnp.concatenate([jnp.where(head == hd, q, jnp.zeros_like(q)) for hd in range(N_HEADS)], axis=0)
        sc = lax.dot_general(qs, kw, _NT, preferred_element_type=F32)
        sc = jnp.where(valid, sc, NEG)
        mx = jnp.max(sc, axis=1, keepdims=True)
        p = jnp.exp(sc - mx)
        sm = jnp.sum(p, axis=1, keepdims=True)
        oh = jnp.dot((p * (1.0 / sm)).astype(BF16), vw, preferred_element_type=F32)
        lse = mx + jnp.log(sm)
        o = jnp.zeros((L, BRANCH_WIDTH), F32)
        lw = jnp.zeros((L, BRANCH_WIDTH), F32)
        for hd in range(N_HEADS):
            o = jnp.where(head == hd, oh[hd * L:(hd + 1) * L], o)
            lw = jnp.where(head == hd, lse[hd * L:(hd + 1) * L], lw)
        idx = rows_of(q_start, d)
        for p_ in range(halves):
            og_ref[gi, p_, idx, :] = o[:, p_ * LANES:(p_ + 1) * LANES]
            lg_ref[gi, p_, idx, :] = lw[:, p_ * LANES:(p_ + 1) * LANES]

    def group(gi, d):
        nb = C // (L * d)

        def first(r, carry):
            start = (nb - 1) * L * d + r
            unit(gi, d, r, load(kp_ref, 0, start, d), load(vp_ref, 0, start, d), band_first)
            return carry

        lax.fori_loop(0, d, first, 0, unroll=_small_unroll(d))

        def rest(idx, carry):
            start = (1 + idx // d) * (L * d) + idx % d
            prev = start - L * d
            unit(gi, d, start, load(cur_ref, halves, prev, d), load(cur_ref, 2 * halves, prev, d), band)
            return carry

        if nb > 1:
            lax.fori_loop(0, d * (nb - 1), rest, 0, unroll=_small_unroll(d * (nb - 1)))

    for gi, (_, d) in enumerate(DIL_CONFIGS):
        pl.when(g == gi)(functools.partial(group, gi, d))

    @pl.when(g == N_DIL - 1)
    def _():
        for p_ in range(halves):
            lgs = [lg_ref[gi, p_] for gi in range(N_DIL)]
            mx = functools.reduce(jnp.maximum, lgs)
            es = [jnp.exp(l - mx) for l in lgs]
            num = functools.reduce(lambda u, w: u + w, [es[gi] * og_ref[gi, p_] for gi in range(N_DIL)])
            den = functools.reduce(lambda u, w: u + w, es)
            out_ref[:, p_ * LANES:(p_ + 1) * LANES] = (num / den).astype(BF16)


def _dil_attention(dil):
    b, slabs, s, _ = dil.shape
    C = DIL_CHUNK
    assert slabs == DIL_SLABS and s % C == 0 and all(w // d == DIL_L and C % w == 0 for w, d in DIL_CONFIGS)
    per_group = DIL_SLABS // N_DIL
    halves = BRANCH_WIDTH // LANES

    def prev(t):
        return pl.BlockSpec((None, halves, C, LANES),
                            lambda bi, ch, g: (bi, 3 * g + t, jnp.maximum(ch - 1, 0), 0))

    scratch = pltpu.VMEM((N_DIL, halves, C, LANES), F32)
    return pl.pallas_call(
        _dil_kernel,
        grid=(b, s // C, N_DIL),
        in_specs=[pl.BlockSpec((None, per_group, C, LANES), lambda bi, ch, g: (bi, g, ch, 0)), prev(1), prev(2)],
        out_specs=pl.BlockSpec((None, C, BRANCH_WIDTH), lambda bi, ch, g: (bi, ch, 0)),
        out_shape=jax.ShapeDtypeStruct((b, s, BRANCH_WIDTH), BF16),
        scratch_shapes=[scratch] * 2,
        compiler_params=_cparams("parallel", "parallel", "arbitrary"),
        name="dil_attention",
    )(dil, dil, dil)


def _merge_kernel(x_ref, oa_ref, ob_ref, oc_ref, g_ref, wa_ref, wb_ref, wc_ref, wo_ref, out_ref):
    merged = g_ref[:, :D_MODEL].astype(F32) * jnp.dot(oa_ref[...], wa_ref[...], preferred_element_type=F32)
    merged += g_ref[:, D_MODEL:2 * D_MODEL].astype(F32) * jnp.dot(ob_ref[...], wb_ref[...], preferred_element_type=F32)
    merged += g_ref[:, 2 * D_MODEL:].astype(F32) * jnp.dot(oc_ref[...], wc_ref[...], preferred_element_type=F32)
    out_ref[...] = x_ref[...] + jnp.dot(merged.astype(BF16), wo_ref[...], preferred_element_type=F32)


def _merge(x, oa, ob, oc, gates, wa, wb, wc, wo, tm=512):
    t = x.shape[0]
    row = lambda w: pl.BlockSpec((tm, w), lambda i: (i, 0))
    full = lambda a: pl.BlockSpec(a.shape, lambda i: (0,) * a.ndim)
    return pl.pallas_call(
        _merge_kernel,
        grid=(t // tm,),
        in_specs=[row(D_MODEL), row(BRANCH_WIDTH), row(N_HEADS * LANES), row(BRANCH_WIDTH), row(GATE_COLS),
                  full(wa), full(wb), full(wc), full(wo)],
        out_specs=row(D_MODEL),
        out_shape=jax.ShapeDtypeStruct((t, D_MODEL), F32),
        compiler_params=_cparams("parallel"),
        name="merge",
    )(x, oa, ob, oc, gates, wa, wb, wc, wo)


def _route(logits):
    col = lambda k: logits[:, k:k + 1]
    gl = [col(k) for k in range(N_GROUPS)]
    gmax = functools.reduce(jnp.maximum, gl)
    taken = jnp.zeros_like(gmax) > 1.0
    is_g = []
    for k in range(N_GROUPS):
        hit = jnp.logical_and(gl[k] == gmax, jnp.logical_not(taken))
        is_g.append(hit)
        taken = jnp.logical_or(taken, hit)
    p_group = 1.0 / functools.reduce(lambda u, w: u + w, [jnp.exp(g - gmax) for g in gl])
    a = []
    for e in range(EXPERTS_PER_GROUP):
        v = col(N_GROUPS + (N_GROUPS - 1) * EXPERTS_PER_GROUP + e)
        for k in range(N_GROUPS - 2, -1, -1):
            v = jnp.where(is_g[k], col(N_GROUPS + k * EXPERTS_PER_GROUP + e), v)
        a.append(v)
    t1 = functools.reduce(jnp.maximum, a)
    taken = jnp.zeros_like(t1) > 1.0
    first = []
    for e in range(EXPERTS_PER_GROUP):
        hit = jnp.logical_and(a[e] == t1, jnp.logical_not(taken))
        first.append(hit)
        taken = jnp.logical_or(taken, hit)
    rest = [jnp.where(first[e], -jnp.inf, a[e]) for e in range(EXPERTS_PER_GROUP)]
    t2 = functools.reduce(jnp.maximum, rest)
    taken = jnp.zeros_like(t1) > 1.0
    second = []
    for e in range(EXPERTS_PER_GROUP):
        hit = jnp.logical_and(jnp.logical_and(rest[e] == t2, jnp.logical_not(first[e])), jnp.logical_not(taken))
        second.append(hit)
        taken = jnp.logical_or(taken, hit)
    d = jnp.exp(t2 - t1)
    p1 = 1.0 / (1.0 + d)
    p2 = d / (1.0 + d)
    lane = lax.broadcasted_iota(jnp.int32, (1, LANES), 1)
    w_lane = jnp.zeros(logits.shape, F32)
    for e in range(EXPERTS_PER_GROUP):
        w_e = p_group * jnp.where(first[e], p1, jnp.where(second[e], p2, 0.0))
        w_lane = jnp.where(lane == e, w_e, w_lane)
    return [jnp.where(is_g[k], w_lane, 0.0) for k in range(N_GROUPS)]


def _moe_kernel(x_ref, g_ref, wr_hi_ref, wr_lo_ref, br_ref, wg_ref, wu_ref, wd_ref, gfin_ref, out_ref,
                h_ref, comb_ref, acc_ref, *, final):
    gi = pl.program_id(1)

    @pl.when(gi == 0)
    def _():
        x = x_ref[...]
        hf = _rms(x, g_ref[...])
        h_hi = hf.astype(BF16)
        h_lo = (hf - h_hi.astype(F32)).astype(BF16)
        logits = (jnp.dot(h_hi, wr_hi_ref[...], preferred_element_type=F32)
                  + jnp.dot(h_lo, wr_hi_ref[...], preferred_element_type=F32)
                  + jnp.dot(h_hi, wr_lo_ref[...], preferred_element_type=F32)) + br_ref[...]
        comb = _route(logits)
        for k in range(N_GROUPS):
            comb_ref[k] = comb[k]
        h_ref[...] = h_hi
        acc_ref[...] = x

    h = h_ref[...]
    gate = jnp.dot(h, wg_ref[...], preferred_element_type=F32)
    up = jnp.dot(h, wu_ref[...], preferred_element_type=F32)
    comb = comb_ref[gi]
    hid = gate * (1.0 / (1.0 + jnp.exp(-gate))) * up
    parts = [hid[:, e * D_EXPERT:(e + 1) * D_EXPERT] * comb[:, e:e + 1] for e in range(EXPERTS_PER_GROUP)]
    hid = jnp.concatenate(parts, axis=1).astype(BF16)
    acc_ref[...] += jnp.dot(hid, wd_ref[...], preferred_element_type=F32)

    @pl.when(gi == N_GROUPS - 1)
    def _():
        y = acc_ref[...]
        out_ref[...] = _rms(y, gfin_ref[...]) if final else y


def _moe(x, g_ffn, wr_hi, wr_lo, br, wg, wu, wd, g_final, final, tm=512):
    t = x.shape[0]
    row = pl.BlockSpec((tm, D_MODEL), lambda i, k: (i, 0))
    full = lambda a: pl.BlockSpec(a.shape, lambda i, k: (0,) * a.ndim)
    grp = pl.BlockSpec((None, D_MODEL, D_MODEL), lambda i, k: (k, 0, 0))
    return pl.pallas_call(
        functools.partial(_moe_kernel, final=final),
        grid=(t // tm, N_GROUPS),
        in_specs=[row, full(g_ffn), full(wr_hi), full(wr_lo), full(br), grp, grp, grp, full(g_final)],
        out_specs=row,
        out_shape=jax.ShapeDtypeStruct((t, D_MODEL), F32),
        scratch_shapes=[pltpu.VMEM((tm, D_MODEL), BF16), pltpu.VMEM((N_GROUPS, tm, LANES), F32),
                        pltpu.VMEM((tm, D_MODEL), F32)],
        compiler_params=_cparams("parallel", "arbitrary"),
        name="moe",
    )(x, g_ffn, wr_hi, wr_lo, br, wg, wu, wd, g_final)


def _pad_heads(w, per_head, keep):
    lead = w.shape[:-1]
    w = w.reshape(*lead, N_HEADS, per_head)[..., keep]
    w = jnp.pad(w, [(0, 0)] * (w.ndim - 1) + [(0, LANES - w.shape[-1])])
    return w.reshape(*lead, N_HEADS * LANES)


def _prepare_weights(w_in, w_uq, w_ukv, w_branch, w_out, w_group_router, b_group_router, w_expert_router,
                     b_expert_router, w_gate, w_up, w_down):
    depth = w_in.shape[0]
    q_scale = HEAD_DIM ** -0.5
    sb_scale = jnp.concatenate([jnp.full((BRANCH_WIDTH,), q_scale, F32), jnp.ones((2 * BRANCH_WIDTH,), F32)])
    o = 0
    wsb = (w_in[:, :, o:o + SB_COLS] * sb_scale).astype(BF16)
    o += SB_COLS
    w_lat = w_in[:, :, o:o + MLA_Q_LORA + MLA_KV_LORA]
    o += MLA_Q_LORA + MLA_KV_LORA
    w_kr = w_in[:, :, o:o + MLA_ROPE]
    o += MLA_ROPE
    w_kr = jnp.pad(w_kr, ((0, 0), (0, 0), (MLA_NOPE, LANES - MLA_NOPE - MLA_ROPE)))
    wmla = jnp.concatenate([w_lat, w_kr], axis=2).astype(BF16)
    dil_scale = jnp.tile(sb_scale, N_DIL)
    wdil = (w_in[:, :, o:o + DIL_COLS] * dil_scale).astype(BF16)
    o += DIL_COLS
    wg = w_in[:, :, o:].astype(BF16)
    wuq = _pad_heads(w_uq, MLA_NOPE + MLA_ROPE, slice(None)).astype(BF16)
    wuk = _pad_heads(w_ukv, 2 * MLA_NOPE, slice(0, MLA_NOPE)).astype(BF16)
    wuv = _pad_heads(w_ukv, 2 * MLA_NOPE, slice(MLA_NOPE, None)).astype(BF16)
    wa = w_branch[:, 0].astype(BF16)
    wb = w_branch[:, 1].reshape(depth, N_HEADS, HEAD_DIM, D_MODEL)
    wb = jnp.pad(wb, ((0, 0), (0, 0), (0, LANES - HEAD_DIM), (0, 0))).reshape(depth, N_HEADS * LANES, D_MODEL)
    wb = wb.astype(BF16)
    wc = w_branch[:, 2].astype(BF16)
    wo = w_out.astype(BF16)
    n_exp = N_GROUPS * EXPERTS_PER_GROUP
    wr = jnp.concatenate([w_group_router, w_expert_router], axis=2)
    wr = jnp.pad(wr, ((0, 0), (0, 0), (0, LANES - N_GROUPS - n_exp)))
    wr_hi = wr.astype(BF16)
    wr_lo = (wr - wr_hi.astype(F32)).astype(BF16)
    br = jnp.concatenate([b_group_router, b_expert_router], axis=1)
    br = jnp.pad(br, ((0, 0), (0, LANES - N_GROUPS - n_exp)))[:, None, :]

    def by_group(w):
        w = w.reshape(depth, N_GROUPS, EXPERTS_PER_GROUP, D_MODEL, D_EXPERT)
        return w.transpose(0, 1, 3, 2, 4).reshape(depth, N_GROUPS, D_MODEL, EXPERTS_PER_GROUP * D_EXPERT).astype(BF16)

    wgate, wup = by_group(w_gate), by_group(w_up)
    wdown = w_down.reshape(depth, N_GROUPS, EXPERTS_PER_GROUP * D_EXPERT, D_MODEL).astype(BF16)
    return dict(wsb=wsb, wmla=wmla, wdil=wdil, wg=wg, wuq=wuq, wuk=wuk, wuv=wuv, wa=wa, wb=wb, wc=wc, wo=wo,
                wr_hi=wr_hi, wr_lo=wr_lo, br=br, wgate=wgate, wup=wup, wdown=wdown)


def kernel(x, positions, g_mix, w_in, g_q_lat, w_uq, g_kv_lat, w_ukv, w_branch, w_out, g_ffn, w_group_router, b_group_router, w_expert_router, b_expert_router, w_gate, w_up, w_down, g_final):
    b, s, d = x.shape
    depth = w_in.shape[0]
    assert d == D_MODEL and w_in.shape[2] == SB_COLS + MLA_COLS + DIL_COLS + GATE_COLS
    t = b * s
    w = _prepare_weights(w_in, w_uq, w_ukv, w_branch, w_out, w_group_router, b_group_router, w_expert_router,
                         b_expert_router, w_gate, w_up, w_down)
    tabs = _rope_tables(positions.reshape(t, 1).astype(F32))
    xf = x.reshape(t, d)
    g_fin = g_final.reshape(1, d)
    for l in range(depth):
        sb, m, dil, gates = _proj(xf, b, g_mix[l][None], w["wsb"][l], w["wmla"][l], w["wdil"][l], w["wg"][l],
                                  g_q_lat[l][None], w["wuq"][l], g_kv_lat[l][None], w["wuk"][l], w["wuv"][l], tabs)
        oa = _sb_attention(sb.reshape(b, s, -1)).reshape(t, -1)
        ob = _mla_attention(m.reshape(b, s, -1)).reshape(t, -1)
        oc = _dil_attention(dil).reshape(t, -1)
        xf = _merge(xf, oa, ob, oc, gates, w["wa"][l], w["wb"][l], w["wc"][l], w["wo"][l])
        xf = _moe(xf, g_ffn[l][None], w["wr_hi"][l], w["wr_lo"][l], w["br"][l], w["wgate"][l], w["wup"][l],
                  w["wdown"][l], g_fin, final=(l == depth - 1))
    return xf.reshape(b, s, d)
```

```python
import functools
import math

import jax
import jax.numpy as jnp
from jax import lax
from jax.experimental import pallas as pl
from jax.experimental.pallas import tpu as pltpu

F32 = jnp.float32
BF16 = jnp.bfloat16

D_MODEL = 1024
HEAD_DIM = 64
N_HEADS = 4
BRANCH_WIDTH = N_HEADS * HEAD_DIM
ROPE_THETA = 10000.0
EPS = 1e-6
NEG = -1e30
MLA_NOPE = 64
MLA_ROPE = 32
MLA_Q_LORA = 256
MLA_KV_LORA = 128
DIL_CONFIGS = ((128, 1), (512, 4), (2048, 16))
DIL_L = 128
N_DIL = len(DIL_CONFIGS)
N_BRANCH = 3
N_GROUPS = 4
EXPERTS_PER_GROUP = 4
D_EXPERT = 256
ROUTER_ROWS = 32
SB_COLS = 3 * BRANCH_WIDTH
MLA_COLS = MLA_Q_LORA + MLA_KV_LORA + MLA_ROPE
DIL_COLS = N_DIL * 3 * BRANCH_WIDTH
GATE_COLS = N_BRANCH * D_MODEL
MLA_V = 64
LOG2_E = math.log2(math.e)

LANES = 128
VMEM_LIMIT = 56 * 1024 * 1024
DIL_SLABS = DIL_COLS // LANES
DIL_CHUNK = DIL_CONFIGS[-1][0]

SB_STOP = -110.0

_NT = (((1,), (1,)), ((), ()))


def _cparams(*sem):
    return pltpu.CompilerParams(dimension_semantics=sem, vmem_limit_bytes=VMEM_LIMIT)


def _head_of_lane(width, per_head):
    return lax.broadcasted_iota(jnp.int32, (1, width), 1) // per_head


def _rope_table_kernel(pos_ref, dc_ref, dsa_ref, dsb_ref, mc_ref, msa_ref, msb_ref):
    pos = pos_ref[...]
    lane = lax.broadcasted_iota(jnp.int32, (1, LANES), 1)

    def tables(dim, in_rope, first_half, freq_idx):
        inv_freq = jnp.exp(freq_idx.astype(F32) * (-2.0 / dim * math.log(ROPE_THETA)))
        ang = pos * inv_freq
        cos, sin = jnp.cos(ang), jnp.sin(ang)
        c = jnp.where(in_rope, cos, 1.0)
        sa = jnp.where(in_rope & first_half, -sin, 0.0)
        sb = jnp.where(in_rope & jnp.logical_not(first_half), sin, 0.0)
        return c, sa, sb

    half = HEAD_DIM // 2
    c, sa, sb = tables(HEAD_DIM, lane >= 0, (lane % HEAD_DIM) < half, lane % half)
    dc_ref[...], dsa_ref[...], dsb_ref[...] = c, sa, sb
    half = MLA_ROPE // 2
    rel = lane - MLA_NOPE
    in_rope = (rel >= 0) & (rel < MLA_ROPE)
    c, sa, sb = tables(MLA_ROPE, in_rope, rel < half, jnp.where(in_rope, rel % half, 0))
    mc_ref[...], msa_ref[...], msb_ref[...] = c, sa, sb


def _rope_tables(pos_col, tm=1024):
    t = pos_col.shape[0]
    spec = pl.BlockSpec((tm, LANES), lambda i: (i, 0))
    return pl.pallas_call(
        _rope_table_kernel,
        grid=(t // tm,),
        in_specs=[pl.BlockSpec((tm, 1), lambda i: (i, 0))],
        out_specs=[spec] * 6,
        out_shape=[jax.ShapeDtypeStruct((t, LANES), F32)] * 6,
        compiler_params=_cparams("parallel"),
        name="rope_tables",
    )(pos_col)


def _rope_block(x, c, sa, sb, half):
    return x * c + pltpu.roll(x, LANES - half, 1) * sa + pltpu.roll(x, half, 1) * sb


def _rms(x, g):
    return x * lax.rsqrt(jnp.mean(x * x, axis=-1, keepdims=True) + EPS) * g


def _proj_kernel(x_ref, gmix_ref, wsb_ref, wmla_ref, wdil_ref, wg_ref, gq_ref, wuq_ref, gkv_ref, wuk_ref,
                 wuv_ref, dc_ref, dsa_ref, dsb_ref, mc_ref, msa_ref, msb_ref,
                 sb_ref, m_ref, dil_ref, g_ref):
    h = _rms(x_ref[...], gmix_ref[...]).astype(BF16)

    def mm(a, w):
        return jnp.dot(a, w, preferred_element_type=F32)

    for c in range(0, SB_COLS, BRANCH_WIDTH):
        sb_ref[:, c:c + BRANCH_WIDTH] = mm(h, wsb_ref[:, c:c + BRANCH_WIDTH]).astype(BF16)

    lat = mm(h, wmla_ref[...])
    cq = lat[:, :MLA_Q_LORA]
    ckv = lat[:, MLA_Q_LORA:MLA_Q_LORA + MLA_KV_LORA]
    kr = lat[:, MLA_Q_LORA + MLA_KV_LORA:]
    mc, msa, msb = mc_ref[...], msa_ref[...], msb_ref[...]
    kr = _rope_block(kr, mc, msa, msb, MLA_ROPE // 2)
    q = mm(_rms(cq, gq_ref[...]).astype(BF16), wuq_ref[...])
    ckv_n = _rms(ckv, gkv_ref[...]).astype(BF16)
    kn = mm(ckv_n, wuk_ref[...])
    vv = mm(ckv_n, wuv_ref[...])
    q_scale = (MLA_NOPE + MLA_ROPE) ** -0.5 * LOG2_E
    for hd in range(N_HEADS):
        sl = slice(hd * LANES, (hd + 1) * LANES)
        qh = _rope_block(q[:, sl], mc, msa, msb, MLA_ROPE // 2) * q_scale
        m_ref[:, sl] = qh.astype(BF16)
        m_ref[:, N_HEADS * LANES + hd * LANES:N_HEADS * LANES + (hd + 1) * LANES] = (kn[:, sl] + kr).astype(BF16)
    lane = lax.broadcasted_iota(jnp.int32, (1, N_HEADS * LANES), 1)
    m_ref[:, 2 * N_HEADS * LANES:] = jnp.where(lane % LANES == MLA_V, 1.0, vv).astype(BF16)

    dc, dsa, dsb = dc_ref[...], dsa_ref[...], dsb_ref[...]
    for blk in range(N_DIL * 3):
        c0 = blk * BRANCH_WIDTH
        r = mm(h, wdil_ref[:, c0:c0 + BRANCH_WIDTH])
        for p in range(BRANCH_WIDTH // LANES):
            rp = r[:, p * LANES:(p + 1) * LANES]
            if blk % 3 != 2:
                rp = _rope_block(rp, dc, dsa, dsb, HEAD_DIM // 2)
            dil_ref[blk * (BRANCH_WIDTH // LANES) + p] = rp

    gc = 512
    for c in range(0, GATE_COLS, gc):
        z = mm(h, wg_ref[:, c:c + gc])
        g_ref[:, c:c + gc] = (1.0 / (1.0 + jnp.exp(-z))).astype(BF16)


def _proj(x, batch, gmix, wsb, wmla, wdil, wg, gq, wuq, gkv, wuk, wuv, tabs, tm=256):
    t = x.shape[0]
    per_batch = t // batch // tm
    row = lambda w: pl.BlockSpec((tm, w), lambda i: (i, 0))
    full = lambda a: pl.BlockSpec(a.shape, lambda i: (0,) * a.ndim)
    weights = (gmix, wsb, wmla, wdil, wg, gq, wuq, gkv, wuk, wuv)
    m_cols = 3 * N_HEADS * LANES
    dil_spec = pl.BlockSpec((None, DIL_SLABS, tm, LANES), lambda i: (i // per_batch, 0, i % per_batch, 0))
    return pl.pallas_call(
        _proj_kernel,
        grid=(t // tm,),
        in_specs=[row(D_MODEL)] + [full(w) for w in weights] + [row(LANES)] * 6,
        out_specs=[row(SB_COLS), row(m_cols), dil_spec, row(GATE_COLS)],
        out_shape=[jax.ShapeDtypeStruct((t, SB_COLS), BF16), jax.ShapeDtypeStruct((t, m_cols), BF16),
                   jax.ShapeDtypeStruct((batch, DIL_SLABS, t // batch, LANES), F32),
                   jax.ShapeDtypeStruct((t, GATE_COLS), BF16)],
        compiler_params=_cparams("parallel"),
        name="proj",
    )(x, *weights, *tabs)


def _sb_kernel(q_ref, k_ref, v_ref, o_ref, acc_ref, c_ref, *, tq):
    i = pl.program_id(1)
    q = q_ref[...]
    head = _head_of_lane(BRANCH_WIDTH, HEAD_DIM)
    qs = jnp.concatenate([jnp.where(head == hd, q, jnp.zeros_like(q)) for hd in range(N_HEADS)], axis=0)
    rows = N_HEADS * tq
    krow = lax.broadcasted_iota(jnp.int32, (tq, tq + LANES), 0)
    kcol = lax.broadcasted_iota(jnp.int32, (tq, tq + LANES), 1)
    later_ones = jnp.where((krow > kcol) | (kcol >= tq), 1.0, 0.0).astype(BF16)
    before = (lax.broadcasted_iota(jnp.int32, (rows, tq), 1)
              < lax.broadcasted_iota(jnp.int32, (rows, tq), 0) % tq)
    acc_ref[...] = jnp.zeros_like(acc_ref)
    c_ref[...] = jnp.zeros_like(c_ref)

    def block(j, diagonal):
        k = k_ref[pl.ds(pl.multiple_of(j * tq, tq), tq), :]
        v = v_ref[pl.ds(pl.multiple_of(j * tq, tq), tq), :]
        z = lax.dot_general(qs, k, _NT, preferred_element_type=F32)
        sp = jnp.log(1.0 + jnp.exp(-jnp.abs(z)))
        log_beta = jnp.minimum(z, 0.0) - sp
        log_1m = log_beta - z
        if diagonal:
            log_1m = jnp.where(before, log_1m, 0.0)
        sums = jnp.dot(log_1m.astype(BF16), later_ones, preferred_element_type=F32)
        c = c_ref[...]
        w = jnp.exp(log_beta + jnp.concatenate([c] * (tq // LANES), axis=1) + sums[:, :tq])
        if diagonal:
            w = jnp.where(before, w, 0.0)
        acc_ref[...] += jnp.dot(w.astype(BF16), v, preferred_element_type=F32)
        c_new = c + sums[:, tq:]
        c_ref[...] = c_new
        return jnp.max(c_new)

    first = block(i, True)
    lax.while_loop(lambda cr: (cr[0] >= 0) & (cr[1] > SB_STOP), lambda cr: (cr[0] - 1, block(cr[0], False)),
                   (i - 1, first))
    out = jnp.zeros((tq, BRANCH_WIDTH), F32)
    for hd in range(N_HEADS):
        out = jnp.where(head == hd, acc_ref[hd * tq:(hd + 1) * tq, :], out)
    o_ref[...] = out.astype(BF16)


def _sb_attention(sb, tq=256):
    b, s, _ = sb.shape
    return pl.pallas_call(
        functools.partial(_sb_kernel, tq=tq),
        grid=(b, s // tq),
        in_specs=[pl.BlockSpec((None, tq, BRANCH_WIDTH), lambda bi, i: (bi, i, 0)),
                  pl.BlockSpec((None, s, BRANCH_WIDTH), lambda bi, i: (bi, 0, 1)),
                  pl.BlockSpec((None, s, BRANCH_WIDTH), lambda bi, i: (bi, 0, 2))],
        out_specs=pl.BlockSpec((None, tq, BRANCH_WIDTH), lambda bi, i: (bi, i, 0)),
        out_shape=jax.ShapeDtypeStruct((b, s, BRANCH_WIDTH), BF16),
        scratch_shapes=[pltpu.VMEM((N_HEADS * tq, BRANCH_WIDTH), F32), pltpu.VMEM((N_HEADS * tq, LANES), F32)],
        compiler_params=_cparams("parallel", "parallel"),
        name="sb_attention",
    )(sb, sb, sb)


def _mla_kernel(q_ref, k_ref, v_ref, o_ref, m_ref, acc_ref, *, t):
    i = pl.program_id(1)
    row = lax.broadcasted_iota(jnp.int32, (t, t), 0)
    col = lax.broadcasted_iota(jnp.int32, (t, t), 1)
    m_ref[...] = jnp.full(m_ref.shape, NEG, F32)
    acc_ref[...] = jnp.zeros_like(acc_ref)

    def step(j, masked):
        rows = pl.ds(pl.multiple_of(j * t, t), t)
        for hd in range(N_HEADS):
            sl = slice(hd * LANES, (hd + 1) * LANES)
            s = lax.dot_general(q_ref[:, sl], k_ref[rows, sl], _NT, preferred_element_type=F32)
            if masked:
                s = jnp.where(col <= row, s, NEG)
            m_prev = m_ref[hd]
            m_new = jnp.maximum(m_prev, jnp.max(s, axis=1, keepdims=True))
            p = jnp.exp2((s - jnp.concatenate([m_new] * (t // LANES), axis=1)).astype(BF16))
            acc_ref[hd] = jnp.exp2(m_prev - m_new) * acc_ref[hd] + jnp.dot(p, v_ref[rows, sl],
                                                                           preferred_element_type=F32)
            m_ref[hd] = m_new

    def body(j, carry):
        step(j, False)
        return carry

    lax.fori_loop(0, i, body, 0)
    step(i, True)
    for hd in range(N_HEADS):
        acc = acc_ref[hd]
        o_ref[:, hd * LANES:(hd + 1) * LANES] = (acc / acc[:, MLA_V:MLA_V + 1]).astype(BF16)


def _mla_attention(m, t=512):
    b, s, _ = m.shape
    w = N_HEADS * LANES
    t = min(t, s)
    return pl.pallas_call(
        functools.partial(_mla_kernel, t=t),
        grid=(b, s // t),
        in_specs=[pl.BlockSpec((None, t, w), lambda bi, i: (bi, i, 0)),
                  pl.BlockSpec((None, s, w), lambda bi, i: (bi, 0, 1)),
                  pl.BlockSpec((None, s, w), lambda bi, i: (bi, 0, 2))],
        out_specs=pl.BlockSpec((None, t, w), lambda bi, i: (bi, i, 0)),
        out_shape=jax.ShapeDtypeStruct((b, s, w), BF16),
        scratch_shapes=[pltpu.VMEM((N_HEADS, t, LANES), F32), pltpu.VMEM((N_HEADS, t, LANES), F32)],
        compiler_params=_cparams("parallel", "parallel"),
        name="mla_attention",
    )(m, m, m)


def _small_unroll(n):
    return next((u for u in (4, 5, 3, 2) if n % u == 0), 1)


def _dil_kernel(cur_ref, kp_ref, vp_ref, out_ref, og_ref, lg_ref):
    g = pl.program_id(2)
    has_prev_chunk = pl.program_id(1) > 0
    L, C = DIL_L, DIL_CHUNK
    halves = BRANCH_WIDTH // LANES
    head = _head_of_lane(BRANCH_WIDTH, HEAD_DIM)
    a = lax.broadcasted_iota(jnp.int32, (N_HEADS * L, 2 * L), 0) % L
    c = lax.broadcasted_iota(jnp.int32, (N_HEADS * L, 2 * L), 1)
    band = (c >= a) & (c <= a + L)
    band_first = band & ((c >= L) | has_prev_chunk)

    def rows_of(start, d):
        return pl.ds(start, L) if d == 1 else pl.ds(start, L, stride=d)

    def load(ref, slab, start, d):
        idx = rows_of(start, d)
        return jnp.concatenate([ref[slab + p, idx, :] for p in range(halves)], axis=1).astype(BF16)

    def unit(gi, d, q_start, kprev, vprev, valid):
        q = load(cur_ref, 0, q_start, d)
        kw = jnp.concatenate([kprev, load(cur_ref, halves, q_start, d)], axis=0)
        vw = jnp.concatenate([vprev, load(cur_ref, 2 * halves, q_start, d)], axis=0)
        qs = jnp.concatenate([jnp.where(head == hd, q, jnp.zeros_like(q)) for hd in range(N_HEADS)], axis=0)
        sc = lax.dot_general(qs, kw, _NT, preferred_element_type=F32)
        sc = jnp.where(valid, sc, NEG)
        mx = jnp.max(sc, axis=1, keepdims=True)
        p = jnp.exp(sc - mx)
        sm = jnp.sum(p, axis=1, keepdims=True)
        oh = jnp.dot((p * (1.0 / sm)).astype(BF16), vw, preferred_element_type=F32)
        lse = mx + jnp.log(sm)
        o = jnp.zeros((L, BRANCH_WIDTH), F32)
        lw = jnp.zeros((L, BRANCH_WIDTH), F32)
        for hd in range(N_HEADS):
            o = jnp.where(head == hd, oh[hd * L:(hd + 1) * L], o)
            lw = jnp.where(head == hd, lse[hd * L:(hd + 1) * L], lw)
        idx = rows_of(q_start, d)
        for p_ in range(halves):
            og_ref[gi, p_, idx, :] = o[:, p_ * LANES:(p_ + 1) * LANES]
            lg_ref[gi, p_, idx, :] = lw[:, p_ * LANES:(p_ + 1) * LANES]

    def group(gi, d):
        nb = C // (L * d)

        def first(r, carry):
            start = (nb - 1) * L * d + r
            unit(gi, d, r, load(kp_ref, 0, start, d), load(vp_ref, 0, start, d), band_first)
            return carry

        lax.fori_loop(0, d, first, 0, unroll=_small_unroll(d))

        def rest(idx, carry):
            start = (1 + idx // d) * (L * d) + idx % d
            prev = start - L * d
            unit(gi, d, start, load(cur_ref, halves, prev, d), load(cur_ref, 2 * halves, prev, d), band)
            return carry

        if nb > 1:
            lax.fori_loop(0, d * (nb - 1), rest, 0, unroll=_small_unroll(d * (nb - 1)))

    for gi, (_, d) in enumerate(DIL_CONFIGS):
        pl.when(g == gi)(functools.partial(group, gi, d))

    @pl.when(g == N_DIL - 1)
    def _():
        for p_ in range(halves):
            lgs = [lg_ref[gi, p_] for gi in range(N_DIL)]
            mx = functools.reduce(jnp.maximum, lgs)
            es = [jnp.exp(l - mx) for l in lgs]
            num = functools.reduce(lambda u, w: u + w, [es[gi] * og_ref[gi, p_] for gi in range(N_DIL)])
            den = functools.reduce(lambda u, w: u + w, es)
            out_ref[:, p_ * LANES:(p_ + 1) * LANES] = (num / den).astype(BF16)


def _dil_attention(dil):
    b, slabs, s, _ = dil.shape
    C = DIL_CHUNK
    assert slabs == DIL_SLABS and s % C == 0 and all(w // d == DIL_L and C % w == 0 for w, d in DIL_CONFIGS)
    per_group = DIL_SLABS // N_DIL
    halves = BRANCH_WIDTH // LANES

    def prev(t):
        return pl.BlockSpec((None, halves, C, LANES),
                            lambda bi, ch, g: (bi, 3 * g + t, jnp.maximum(ch - 1, 0), 0))

    scratch = pltpu.VMEM((N_DIL, halves, C, LANES), F32)
    return pl.pallas_call(
        _dil_kernel,
        grid=(b, s // C, N_DIL),
        in_specs=[pl.BlockSpec((None, per_group, C, LANES), lambda bi, ch, g: (bi, g, ch, 0)), prev(1), prev(2)],
        out_specs=pl.BlockSpec((None, C, BRANCH_WIDTH), lambda bi, ch, g: (bi, ch, 0)),
        out_shape=jax.ShapeDtypeStruct((b, s, BRANCH_WIDTH), BF16),
        scratch_shapes=[scratch] * 2,
        compiler_params=_cparams("parallel", "parallel", "arbitrary"),
        name="dil_attention",
    )(dil, dil, dil)


def _merge_kernel(x_ref, oa_ref, ob_ref, oc_ref, g_ref, wa_ref, wb_ref, wc_ref, wo_ref, out_ref):
    merged = g_ref[:, :D_MODEL].astype(F32) * jnp.dot(oa_ref[...], wa_ref[...], preferred_element_type=F32)
    merged += g_ref[:, D_MODEL:2 * D_MODEL].astype(F32) * jnp.dot(ob_ref[...], wb_ref[...], preferred_element_type=F32)
    merged += g_ref[:, 2 * D_MODEL:].astype(F32) * jnp.dot(oc_ref[...], wc_ref[...], preferred_element_type=F32)
    out_ref[...] = x_ref[...] + jnp.dot(merged.astype(BF16), wo_ref[...], preferred_element_type=F32)


def _merge(x, oa, ob, oc, gates, wa, wb, wc, wo, tm=512):
    t = x.shape[0]
    row = lambda w: pl.BlockSpec((tm, w), lambda i: (i, 0))
    full = lambda a: pl.BlockSpec(a.shape, lambda i: (0,) * a.ndim)
    return pl.pallas_call(
        _merge_kernel,
        grid=(t // tm,),
        in_specs=[row(D_MODEL), row(BRANCH_WIDTH), row(N_HEADS * LANES), row(BRANCH_WIDTH), row(GATE_COLS),
                  full(wa), full(wb), full(wc), full(wo)],
        out_specs=row(D_MODEL),
        out_shape=jax.ShapeDtypeStruct((t, D_MODEL), F32),
        compiler_params=_cparams("parallel"),
        name="merge",
    )(x, oa, ob, oc, gates, wa, wb, wc, wo)


def _route(logits_t):
    col = lambda k: logits_t[k:k + 1, :]
    gl = [col(k) for k in range(N_GROUPS)]
    gmax = functools.reduce(jnp.maximum, gl)
    taken = jnp.zeros_like(gmax) > 1.0
    is_g = []
    for k in range(N_GROUPS):
        hit = jnp.logical_and(gl[k] == gmax, jnp.logical_not(taken))
        is_g.append(hit)
        taken = jnp.logical_or(taken, hit)
    p_group = 1.0 / functools.reduce(lambda u, w: u + w, [jnp.exp(g - gmax) for g in gl])
    a = []
    for e in range(EXPERTS_PER_GROUP):
        v = col(N_GROUPS + (N_GROUPS - 1) * EXPERTS_PER_GROUP + e)
        for k in range(N_GROUPS - 2, -1, -1):
            v = jnp.where(is_g[k], col(N_GROUPS + k * EXPERTS_PER_GROUP + e), v)
        a.append(v)
    t1 = functools.reduce(jnp.maximum, a)
    taken = jnp.zeros_like(t1) > 1.0
    first = []
    for e in range(EXPERTS_PER_GROUP):
        hit = jnp.logical_and(a[e] == t1, jnp.logical_not(taken))
        first.append(hit)
        taken = jnp.logical_or(taken, hit)
    rest = [jnp.where(first[e], -jnp.inf, a[e]) for e in range(EXPERTS_PER_GROUP)]
    t2 = functools.reduce(jnp.maximum, rest)
    taken = jnp.zeros_like(t1) > 1.0
    second = []
    for e in range(EXPERTS_PER_GROUP):
        hit = jnp.logical_and(jnp.logical_and(rest[e] == t2, jnp.logical_not(first[e])), jnp.logical_not(taken))
        second.append(hit)
        taken = jnp.logical_or(taken, hit)
    d = jnp.exp(t2 - t1)
    p1 = 1.0 / (1.0 + d)
    p2 = d / (1.0 + d)
    w = [p_group * jnp.where(first[e], p1, jnp.where(second[e], p2, 0.0)) for e in range(EXPERTS_PER_GROUP)]
    return [[jnp.where(is_g[k], w[e], 0.0) for e in range(EXPERTS_PER_GROUP)] for k in range(N_GROUPS)]


def _moe_kernel(x_ref, g_ref, wr_ref, br_ref, wg_ref, wu_ref, wd_ref, gfin_ref, out_ref,
                h_ref, comb_ref, acc_ref, *, final):
    gi = pl.program_id(1)

    @pl.when(gi == 0)
    def _():
        x = x_ref[...]
        hf = _rms(x, g_ref[...])
        h_hi = hf.astype(BF16)
        h_lo = (hf - h_hi.astype(F32)).astype(BF16)
        hi = lax.dot_general(wr_ref[...], h_hi, _NT, preferred_element_type=F32)
        lo = lax.dot_general(wr_ref[:ROUTER_ROWS, :], h_lo, _NT, preferred_element_type=F32)
        comb = _route(hi[:ROUTER_ROWS] + hi[ROUTER_ROWS:] + lo + br_ref[...])
        sub = lax.broadcasted_iota(jnp.int32, (LANES, 1), 0)
        for k in range(N_GROUPS):
            comb_t = jnp.zeros((LANES, x.shape[0]), F32)
            for e in range(EXPERTS_PER_GROUP):
                comb_t = jnp.where(sub == e, comb[k][e], comb_t)
            comb_ref[k] = comb_t.T
        h_ref[...] = h_hi
        acc_ref[...] = x

    h = h_ref[...]
    comb = comb_ref[gi]
    parts = []
    for e in range(EXPERTS_PER_GROUP):
        gate = jnp.dot(h, wg_ref[e], preferred_element_type=F32)
        up = jnp.dot(h, wu_ref[e], preferred_element_type=F32)
        parts.append((gate * (1.0 / (1.0 + jnp.exp(-gate))) * up * comb[:, e:e + 1]).astype(BF16))
    hid = jnp.concatenate(parts, axis=1)
    acc_ref[...] += jnp.dot(hid, wd_ref[...], preferred_element_type=F32)

    @pl.when(gi == N_GROUPS - 1)
    def _():
        y = acc_ref[...]
        out_ref[...] = _rms(y, gfin_ref[...]) if final else y


def _moe(x, g_ffn, wr, br, wg, wu, wd, g_final, final, tm=512):
    t = x.shape[0]
    row = pl.BlockSpec((tm, D_MODEL), lambda i, k: (i, 0))
    full = lambda a: pl.BlockSpec(a.shape, lambda i, k: (0,) * a.ndim)
    grp_in = pl.BlockSpec((EXPERTS_PER_GROUP, D_MODEL, D_EXPERT), lambda i, k: (k, 0, 0))
    grp_out = pl.BlockSpec((None, EXPERTS_PER_GROUP * D_EXPERT, D_MODEL), lambda i, k: (k, 0, 0))
    return pl.pallas_call(
        functools.partial(_moe_kernel, final=final),
        grid=(t // tm, N_GROUPS),
        in_specs=[row, full(g_ffn), full(wr), full(br), grp_in, grp_in, grp_out, full(g_final)],
        out_specs=row,
        out_shape=jax.ShapeDtypeStruct((t, D_MODEL), F32),
        scratch_shapes=[pltpu.VMEM((tm, D_MODEL), BF16), pltpu.VMEM((N_GROUPS, tm, LANES), F32),
                        pltpu.VMEM((tm, D_MODEL), F32)],
        compiler_params=_cparams("parallel", "arbitrary"),
        name="moe",
    )(x, g_ffn, wr, br, wg, wu, wd, g_final)


def _pad_heads(w, per_head, keep):
    lead = w.shape[:-1]
    w = w.reshape(*lead, N_HEADS, per_head)[..., keep]
    w = jnp.pad(w, [(0, 0)] * (w.ndim - 1) + [(0, LANES - w.shape[-1])])
    return w.reshape(*lead, N_HEADS * LANES)


def _prepare_weights(w_in, w_uq, w_ukv, w_branch, w_out, w_group_router, b_group_router, w_expert_router,
                     b_expert_router, w_gate, w_up, w_down):
    depth = w_in.shape[0]
    q_scale = HEAD_DIM ** -0.5
    sb_scale = jnp.concatenate([jnp.full((BRANCH_WIDTH,), q_scale, F32), jnp.ones((2 * BRANCH_WIDTH,), F32)])
    o = 0
    wsb = (w_in[:, :, o:o + SB_COLS] * sb_scale).astype(BF16)
    o += SB_COLS
    w_lat = w_in[:, :, o:o + MLA_Q_LORA + MLA_KV_LORA]
    o += MLA_Q_LORA + MLA_KV_LORA
    w_kr = w_in[:, :, o:o + MLA_ROPE]
    o += MLA_ROPE
    w_kr = jnp.pad(w_kr, ((0, 0), (0, 0), (MLA_NOPE, LANES - MLA_NOPE - MLA_ROPE)))
    wmla = jnp.concatenate([w_lat, w_kr], axis=2).astype(BF16)
    dil_scale = jnp.tile(sb_scale, N_DIL)
    wdil = (w_in[:, :, o:o + DIL_COLS] * dil_scale).astype(BF16)
    o += DIL_COLS
    wg = w_in[:, :, o:].astype(BF16)
    wuq = _pad_heads(w_uq, MLA_NOPE + MLA_ROPE, slice(None)).astype(BF16)
    wuk = _pad_heads(w_ukv, 2 * MLA_NOPE, slice(0, MLA_NOPE)).astype(BF16)
    wuv = _pad_heads(w_ukv, 2 * MLA_NOPE, slice(MLA_NOPE, None)).astype(BF16)
    wa = w_branch[:, 0].astype(BF16)
    wb = w_branch[:, 1].reshape(depth, N_HEADS, HEAD_DIM, D_MODEL)
    wb = jnp.pad(wb, ((0, 0), (0, 0), (0, LANES - HEAD_DIM), (0, 0))).reshape(depth, N_HEADS * LANES, D_MODEL)
    wb = wb.astype(BF16)
    wc = w_branch[:, 2].astype(BF16)
    wo = w_out.astype(BF16)
    n_exp = N_GROUPS * EXPERTS_PER_GROUP
    wr = jnp.concatenate([w_group_router, w_expert_router], axis=2).transpose(0, 2, 1)
    wr = jnp.pad(wr, ((0, 0), (0, ROUTER_ROWS - N_GROUPS - n_exp), (0, 0)))
    wr_hi = wr.astype(BF16)
    wr = jnp.concatenate([wr_hi, (wr - wr_hi.astype(F32)).astype(BF16)], axis=1)
    br = jnp.concatenate([b_group_router, b_expert_router], axis=1)
    br = jnp.pad(br, ((0, 0), (0, ROUTER_ROWS - N_GROUPS - n_exp)))[:, :, None]

    wgate, wup = w_gate.astype(BF16), w_up.astype(BF16)
    wdown = w_down.astype(BF16).reshape(depth, N_GROUPS, EXPERTS_PER_GROUP * D_EXPERT, D_MODEL)
    return dict(wsb=wsb, wmla=wmla, wdil=wdil, wg=wg, wuq=wuq, wuk=wuk, wuv=wuv, wa=wa, wb=wb, wc=wc, wo=wo,
                wr=wr, br=br, wgate=wgate, wup=wup, wdown=wdown)


def kernel(x, positions, g_mix, w_in, g_q_lat, w_uq, g_kv_lat, w_ukv, w_branch, w_out, g_ffn, w_group_router, b_group_router, w_expert_router, b_expert_router, w_gate, w_up, w_down, g_final):
    b, s, d = x.shape
    depth = w_in.shape[0]
    assert d == D_MODEL and w_in.shape[2] == SB_COLS + MLA_COLS + DIL_COLS + GATE_COLS
    t = b * s
    w = _prepare_weights(w_in, w_uq, w_ukv, w_branch, w_out, w_group_router, b_group_router, w_expert_router,
                         b_expert_router, w_gate, w_up, w_down)
    tabs = _rope_tables(positions.reshape(t, 1).astype(F32))
    xf = x.reshape(t, d)
    g_fin = g_final.reshape(1, d)
    for l in range(depth):
        sb, m, dil, gates = _proj(xf, b, g_mix[l][None], w["wsb"][l], w["wmla"][l], w["wdil"][l], w["wg"][l],
                                  g_q_lat[l][None], w["wuq"][l], g_kv_lat[l][None], w["wuk"][l], w["wuv"][l], tabs)
        oa = _sb_attention(sb.reshape(b, s, -1)).reshape(t, -1)
        ob = _mla_attention(m.reshape(b, s, -1)).reshape(t, -1)
        oc = _dil_attention(dil).reshape(t, -1)
        xf = _merge(xf, oa, ob, oc, gates, w["wa"][l], w["wb"][l], w["wc"][l], w["wo"][l])
        xf = _moe(xf, g_ffn[l][None], w["wr"][l], w["br"][l], w["wgate"][l], w["wup"][l],
                  w["wdown"][l], g_fin, final=(l == depth - 1))
    return xf.reshape(b, s, d)
```

```python
import functools
import math

import jax
import jax.numpy as jnp
from jax import lax
from jax.experimental import pallas as pl
from jax.experimental.pallas import tpu as pltpu

F32 = jnp.float32
BF16 = jnp.bfloat16

D_MODEL = 1024
HEAD_DIM = 64
N_HEADS = 4
BRANCH_WIDTH = N_HEADS * HEAD_DIM
ROPE_THETA = 10000.0
EPS = 1e-6
NEG = -1e30
MLA_NOPE = 64
MLA_ROPE = 32
MLA_Q_LORA = 256
MLA_KV_LORA = 128
DIL_CONFIGS = ((128, 1), (512, 4), (2048, 16))
DIL_L = 128
N_DIL = len(DIL_CONFIGS)
N_BRANCH = 3
N_GROUPS = 4
EXPERTS_PER_GROUP = 4
D_EXPERT = 256
ROUTER_ROWS = 32
SB_COLS = 3 * BRANCH_WIDTH
MLA_COLS = MLA_Q_LORA + MLA_KV_LORA + MLA_ROPE
DIL_COLS = N_DIL * 3 * BRANCH_WIDTH
GATE_COLS = N_BRANCH * D_MODEL
MLA_V = 64
LOG2_E = math.log2(math.e)

LANES = 128
VMEM_LIMIT = 56 * 1024 * 1024
DIL_SLABS = DIL_COLS // LANES
DIL_CHUNK = DIL_CONFIGS[-1][0]

SB_STOP = -110.0
SB_STACKS = 2
MOE_SLOTS = 256

_NT = (((1,), (1,)), ((), ()))


def _cparams(*sem):
    return pltpu.CompilerParams(dimension_semantics=sem, vmem_limit_bytes=VMEM_LIMIT)


def _head_of_lane(width, per_head):
    return lax.broadcasted_iota(jnp.int32, (1, width), 1) // per_head


def _rope_table_kernel(pos_ref, dc_ref, dsa_ref, dsb_ref, mc_ref, msa_ref, msb_ref):
    pos = pos_ref[...]
    lane = lax.broadcasted_iota(jnp.int32, (1, LANES), 1)

    def tables(dim, in_rope, first_half, freq_idx):
        inv_freq = jnp.exp(freq_idx.astype(F32) * (-2.0 / dim * math.log(ROPE_THETA)))
        ang = pos * inv_freq
        cos, sin = jnp.cos(ang), jnp.sin(ang)
        c = jnp.where(in_rope, cos, 1.0)
        sa = jnp.where(in_rope & first_half, -sin, 0.0)
        sb = jnp.where(in_rope & jnp.logical_not(first_half), sin, 0.0)
        return c, sa, sb

    half = HEAD_DIM // 2
    c, sa, sb = tables(HEAD_DIM, lane >= 0, (lane % HEAD_DIM) < half, lane % half)
    dc_ref[...], dsa_ref[...], dsb_ref[...] = c, sa, sb
    half = MLA_ROPE // 2
    rel = lane - MLA_NOPE
    in_rope = (rel >= 0) & (rel < MLA_ROPE)
    c, sa, sb = tables(MLA_ROPE, in_rope, rel < half, jnp.where(in_rope, rel % half, 0))
    mc_ref[...], msa_ref[...], msb_ref[...] = c, sa, sb


def _rope_tables(pos_col, tm=1024):
    t = pos_col.shape[0]
    spec = pl.BlockSpec((tm, LANES), lambda i: (i, 0))
    return pl.pallas_call(
        _rope_table_kernel,
        grid=(t // tm,),
        in_specs=[pl.BlockSpec((tm, 1), lambda i: (i, 0))],
        out_specs=[spec] * 6,
        out_shape=[jax.ShapeDtypeStruct((t, LANES), F32)] * 6,
        compiler_params=_cparams("parallel"),
        name="rope_tables",
    )(pos_col)


def _rope_block(x, c, sa, sb, half):
    return x * c + pltpu.roll(x, LANES - half, 1) * sa + pltpu.roll(x, half, 1) * sb


def _rms(x, g):
    return x * lax.rsqrt(jnp.mean(x * x, axis=-1, keepdims=True) + EPS) * g


def _proj_kernel(x_ref, gmix_ref, wsb_ref, wmla_ref, wdil_ref, wg_ref, gq_ref, wuq_ref, gkv_ref, wuk_ref,
                 wuv_ref, dc_ref, dsa_ref, dsb_ref, mc_ref, msa_ref, msb_ref,
                 sb_ref, m_ref, dil_ref, g_ref):
    h = _rms(x_ref[...], gmix_ref[...]).astype(BF16)

    def mm(a, w):
        return jnp.dot(a, w, preferred_element_type=F32)

    for c in range(0, SB_COLS, BRANCH_WIDTH):
        sb_ref[:, c:c + BRANCH_WIDTH] = mm(h, wsb_ref[:, c:c + BRANCH_WIDTH]).astype(BF16)

    lat = mm(h, wmla_ref[...])
    cq = lat[:, :MLA_Q_LORA]
    ckv = lat[:, MLA_Q_LORA:MLA_Q_LORA + MLA_KV_LORA]
    kr = lat[:, MLA_Q_LORA + MLA_KV_LORA:]
    mc, msa, msb = mc_ref[...], msa_ref[...], msb_ref[...]
    kr = _rope_block(kr, mc, msa, msb, MLA_ROPE // 2)
    q = mm(_rms(cq, gq_ref[...]).astype(BF16), wuq_ref[...])
    ckv_n = _rms(ckv, gkv_ref[...]).astype(BF16)
    kn = mm(ckv_n, wuk_ref[...])
    vv = mm(ckv_n, wuv_ref[...])
    q_scale = (MLA_NOPE + MLA_ROPE) ** -0.5 * LOG2_E
    for hd in range(N_HEADS):
        sl = slice(hd * LANES, (hd + 1) * LANES)
        qh = _rope_block(q[:, sl], mc, msa, msb, MLA_ROPE // 2) * q_scale
        m_ref[:, sl] = qh.astype(BF16)
        m_ref[:, N_HEADS * LANES + hd * LANES:N_HEADS * LANES + (hd + 1) * LANES] = (kn[:, sl] + kr).astype(BF16)
    lane = lax.broadcasted_iota(jnp.int32, (1, N_HEADS * LANES), 1)
    m_ref[:, 2 * N_HEADS * LANES:] = jnp.where(lane % LANES == MLA_V, 1.0, vv).astype(BF16)

    dc, dsa, dsb = dc_ref[...], dsa_ref[...], dsb_ref[...]
    for blk in range(N_DIL * 3):
        c0 = blk * BRANCH_WIDTH
        r = mm(h, wdil_ref[:, c0:c0 + BRANCH_WIDTH])
        for p in range(BRANCH_WIDTH // LANES):
            rp = r[:, p * LANES:(p + 1) * LANES]
            if blk % 3 != 2:
                rp = _rope_block(rp, dc, dsa, dsb, HEAD_DIM // 2)
            dil_ref[blk * (BRANCH_WIDTH // LANES) + p] = rp

    gc = 512
    for c in range(0, GATE_COLS, gc):
        z = mm(h, wg_ref[:, c:c + gc])
        g_ref[:, c:c + gc] = (1.0 / (1.0 + jnp.exp(-z))).astype(BF16)


def _proj(x, batch, gmix, wsb, wmla, wdil, wg, gq, wuq, gkv, wuk, wuv, tabs, tm=512):
    t = x.shape[0]
    per_batch = t // batch // tm
    row = lambda w: pl.BlockSpec((tm, w), lambda i: (i, 0))
    full = lambda a: pl.BlockSpec(a.shape, lambda i: (0,) * a.ndim, pipeline_mode=pl.Buffered(1))
    weights = (gmix, wsb, wmla, wdil, wg, gq, wuq, gkv, wuk, wuv)
    m_cols = 3 * N_HEADS * LANES
    dil_spec = pl.BlockSpec((None, DIL_SLABS, tm, LANES), lambda i: (i // per_batch, 0, i % per_batch, 0))
    return pl.pallas_call(
        _proj_kernel,
        grid=(t // tm,),
        in_specs=[row(D_MODEL)] + [full(w) for w in weights] + [row(LANES)] * 6,
        out_specs=[row(SB_COLS), row(m_cols), dil_spec, row(GATE_COLS)],
        out_shape=[jax.ShapeDtypeStruct((t, SB_COLS), BF16), jax.ShapeDtypeStruct((t, m_cols), BF16),
                   jax.ShapeDtypeStruct((batch, DIL_SLABS, t // batch, LANES), F32),
                   jax.ShapeDtypeStruct((t, GATE_COLS), BF16)],
        compiler_params=_cparams("parallel"),
        name="proj",
    )(x, *weights, *tabs)


def _sb_kernel(q_ref, k_ref, v_ref, o_ref, acc_ref, c_ref, *, tq):
    i = pl.program_id(1)
    q = q_ref[...]
    head = _head_of_lane(BRANCH_WIDTH, HEAD_DIM)
    per_stack = N_HEADS // SB_STACKS
    qs = [jnp.concatenate([jnp.where(head == st * per_stack + hd, q, jnp.zeros_like(q)) for hd in range(per_stack)],
                          axis=0) for st in range(SB_STACKS)]
    rows = per_stack * tq
    krow = lax.broadcasted_iota(jnp.int32, (tq, tq + LANES), 0)
    kcol = lax.broadcasted_iota(jnp.int32, (tq, tq + LANES), 1)
    later_ones = jnp.where((krow > kcol) | (kcol >= tq), 1.0, 0.0).astype(BF16)
    before = (lax.broadcasted_iota(jnp.int32, (rows, tq), 1)
              < lax.broadcasted_iota(jnp.int32, (rows, tq), 0) % tq)
    acc_ref[...] = jnp.zeros_like(acc_ref)
    c_ref[...] = jnp.zeros_like(c_ref)

    def block(j, diagonal):
        k = k_ref[pl.ds(pl.multiple_of(j * tq, tq), tq), :]
        v = v_ref[pl.ds(pl.multiple_of(j * tq, tq), tq), :]
        cmax = jnp.float32(-jnp.inf)
        for st in range(SB_STACKS):
            sl = slice(st * rows, (st + 1) * rows)
            z = lax.dot_general(qs[st], k, _NT, preferred_element_type=F32)
            sp = jnp.log(1.0 + jnp.exp(-jnp.abs(z)))
            log_beta = jnp.minimum(z, 0.0) - sp
            log_1m = log_beta - z
            if diagonal:
                log_1m = jnp.where(before, log_1m, 0.0)
            sums = jnp.dot(log_1m.astype(BF16), later_ones, preferred_element_type=F32)
            c = c_ref[sl, :]
            w = jnp.exp(log_beta + jnp.concatenate([c] * (tq // LANES), axis=1) + sums[:, :tq])
            if diagonal:
                w = jnp.where(before, w, 0.0)
            acc_ref[sl, :] += jnp.dot(w.astype(BF16), v, preferred_element_type=F32)
            c_new = c + sums[:, tq:]
            c_ref[sl, :] = c_new
            cmax = jnp.maximum(cmax, jnp.max(c_new))
        return cmax

    first = block(i, True)
    lax.while_loop(lambda cr: (cr[0] >= 0) & (cr[1] > SB_STOP), lambda cr: (cr[0] - 1, block(cr[0], False)),
                   (i - 1, first))
    out = jnp.zeros((tq, BRANCH_WIDTH), F32)
    for hd in range(N_HEADS):
        out = jnp.where(head == hd, acc_ref[hd * tq:(hd + 1) * tq, :], out)
    o_ref[...] = out.astype(BF16)


def _sb_attention(sb, tq=256):
    b, s, _ = sb.shape
    return pl.pallas_call(
        functools.partial(_sb_kernel, tq=tq),
        grid=(b, s // tq),
        in_specs=[pl.BlockSpec((None, tq, BRANCH_WIDTH), lambda bi, i: (bi, i, 0)),
                  pl.BlockSpec((None, s, BRANCH_WIDTH), lambda bi, i: (bi, 0, 1)),
                  pl.BlockSpec((None, s, BRANCH_WIDTH), lambda bi, i: (bi, 0, 2))],
        out_specs=pl.BlockSpec((None, tq, BRANCH_WIDTH), lambda bi, i: (bi, i, 0)),
        out_shape=jax.ShapeDtypeStruct((b, s, BRANCH_WIDTH), BF16),
        scratch_shapes=[pltpu.VMEM((N_HEADS * tq, BRANCH_WIDTH), F32), pltpu.VMEM((N_HEADS * tq, LANES), F32)],
        compiler_params=_cparams("parallel", "parallel"),
        name="sb_attention",
    )(sb, sb, sb)


def _mla_kernel(q_ref, k_ref, v_ref, o_ref, m_ref, acc_ref, *, t):
    i = pl.program_id(1)
    row = lax.broadcasted_iota(jnp.int32, (t, t), 0)
    col = lax.broadcasted_iota(jnp.int32, (t, t), 1)
    m_ref[...] = jnp.full(m_ref.shape, NEG, F32)
    acc_ref[...] = jnp.zeros_like(acc_ref)

    def step(j, masked):
        rows = pl.ds(pl.multiple_of(j * t, t), t)
        for hd in range(N_HEADS):
            sl = slice(hd * LANES, (hd + 1) * LANES)
            s = lax.dot_general(q_ref[:, sl], k_ref[rows, sl], _NT, preferred_element_type=F32)
            if masked:
                s = jnp.where(col <= row, s, NEG)
            m_prev = m_ref[hd]
            m_new = jnp.maximum(m_prev, jnp.max(s, axis=1, keepdims=True))
            p = jnp.exp2((s - jnp.concatenate([m_new] * (t // LANES), axis=1)).astype(BF16))
            acc_ref[hd] = jnp.exp2(m_prev - m_new) * acc_ref[hd] + jnp.dot(p, v_ref[rows, sl],
                                                                           preferred_element_type=F32)
            m_ref[hd] = m_new

    def pair(j, carry):
        step(2 * j, False)
        step(2 * j + 1, False)
        return carry

    lax.fori_loop(0, i // 2, pair, 0)
    pl.when(i % 2 == 1)(lambda: step(i - 1, False))
    step(i, True)
    for hd in range(N_HEADS):
        acc = acc_ref[hd]
        o_ref[:, hd * LANES:(hd + 1) * LANES] = (acc / acc[:, MLA_V:MLA_V + 1]).astype(BF16)


def _mla_attention(m, t=512):
    b, s, _ = m.shape
    w = N_HEADS * LANES
    t = min(t, s)
    return pl.pallas_call(
        functools.partial(_mla_kernel, t=t),
        grid=(b, s // t),
        in_specs=[pl.BlockSpec((None, t, w), lambda bi, i: (bi, i, 0)),
                  pl.BlockSpec((None, s, w), lambda bi, i: (bi, 0, 1)),
                  pl.BlockSpec((None, s, w), lambda bi, i: (bi, 0, 2))],
        out_specs=pl.BlockSpec((None, t, w), lambda bi, i: (bi, i, 0)),
        out_shape=jax.ShapeDtypeStruct((b, s, w), BF16),
        scratch_shapes=[pltpu.VMEM((N_HEADS, t, LANES), F32), pltpu.VMEM((N_HEADS, t, LANES), F32)],
        compiler_params=_cparams("parallel", "parallel"),
        name="mla_attention",
    )(m, m, m)


def _small_unroll(n):
    return next((u for u in (4, 5, 3, 2) if n % u == 0), 1)


def _dil_kernel(cur_ref, kp_ref, vp_ref, out_ref, og_ref, lg_ref):
    g = pl.program_id(2)
    has_prev_chunk = pl.program_id(1) > 0
    L, C = DIL_L, DIL_CHUNK
    halves = BRANCH_WIDTH // LANES
    head = _head_of_lane(BRANCH_WIDTH, HEAD_DIM)
    a = lax.broadcasted_iota(jnp.int32, (N_HEADS * L, 2 * L), 0) % L
    c = lax.broadcasted_iota(jnp.int32, (N_HEADS * L, 2 * L), 1)
    band = (c >= a) & (c <= a + L)
    band_first = band & ((c >= L) | has_prev_chunk)

    def rows_of(start, d):
        return pl.ds(start, L) if d == 1 else pl.ds(start, L, stride=d)

    def load(ref, slab, start, d):
        idx = rows_of(start, d)
        return jnp.concatenate([ref[slab + p, idx, :] for p in range(halves)], axis=1).astype(BF16)

    def unit(gi, d, q_start, kprev, vprev, valid):
        q = load(cur_ref, 0, q_start, d)
        kw = jnp.concatenate([kprev, load(cur_ref, halves, q_start, d)], axis=0)
        vw = jnp.concatenate([vprev, load(cur_ref, 2 * halves, q_start, d)], axis=0)
        qs = jnp.concatenate([jnp.where(head == hd, q, jnp.zeros_like(q)) for hd in range(N_HEADS)], axis=0)
        sc = lax.dot_general(qs, kw, _NT, preferred_element_type=F32)
        sc = jnp.where(valid, sc, NEG)
        mx = jnp.max(sc, axis=1, keepdims=True)
        p = jnp.exp(sc - mx)
        sm = jnp.sum(p, axis=1, keepdims=True)
        oh = jnp.dot((p * (1.0 / sm)).astype(BF16), vw, preferred_element_type=F32)
        lse = mx + jnp.log(sm)
        o = jnp.zeros((L, BRANCH_WIDTH), F32)
        lw = jnp.zeros((L, BRANCH_WIDTH), F32)
        for hd in range(N_HEADS):
            o = jnp.where(head == hd, oh[hd * L:(hd + 1) * L], o)
            lw = jnp.where(head == hd, lse[hd * L:(hd + 1) * L], lw)
        idx = rows_of(q_start, d)
        for p_ in range(halves):
            og_ref[gi, p_, idx, :] = o[:, p_ * LANES:(p_ + 1) * LANES]
            lg_ref[gi, p_, idx, :] = lw[:, p_ * LANES:(p_ + 1) * LANES]

    def group(gi, d):
        nb = C // (L * d)

        def first(r, carry):
            start = (nb - 1) * L * d + r
            unit(gi, d, r, load(kp_ref, 0, start, d), load(vp_ref, 0, start, d), band_first)
            return carry

        lax.fori_loop(0, d, first, 0, unroll=_small_unroll(d))

        def rest(idx, carry):
            start = (1 + idx // d) * (L * d) + idx % d
            prev = start - L * d
            unit(gi, d, start, load(cur_ref, halves, prev, d), load(cur_ref, 2 * halves, prev, d), band)
            return carry

        if nb > 1:
            lax.fori_loop(0, d * (nb - 1), rest, 0, unroll=_small_unroll(d * (nb - 1)))

    for gi, (_, d) in enumerate(DIL_CONFIGS):
        pl.when(g == gi)(functools.partial(group, gi, d))

    @pl.when(g == N_DIL - 1)
    def _():
        for p_ in range(halves):
            lgs = [lg_ref[gi, p_] for gi in range(N_DIL)]
            mx = functools.reduce(jnp.maximum, lgs)
            es = [jnp.exp(l - mx) for l in lgs]
            num = functools.reduce(lambda u, w: u + w, [es[gi] * og_ref[gi, p_] for gi in range(N_DIL)])
            den = functools.reduce(lambda u, w: u + w, es)
            out_ref[:, p_ * LANES:(p_ + 1) * LANES] = (num / den).astype(BF16)


def _dil_attention(dil):
    b, slabs, s, _ = dil.shape
    C = DIL_CHUNK
    assert slabs == DIL_SLABS and s % C == 0 and all(w // d == DIL_L and C % w == 0 for w, d in DIL_CONFIGS)
    per_group = DIL_SLABS // N_DIL
    halves = BRANCH_WIDTH // LANES

    def prev(t):
        return pl.BlockSpec((None, halves, C, LANES),
                            lambda bi, ch, g: (bi, 3 * g + t, jnp.maximum(ch - 1, 0), 0))

    scratch = pltpu.VMEM((N_DIL, halves, C, LANES), F32)
    return pl.pallas_call(
        _dil_kernel,
        grid=(b, s // C, N_DIL),
        in_specs=[pl.BlockSpec((None, per_group, C, LANES), lambda bi, ch, g: (bi, g, ch, 0)), prev(1), prev(2)],
        out_specs=pl.BlockSpec((None, C, BRANCH_WIDTH), lambda bi, ch, g: (bi, ch, 0)),
        out_shape=jax.ShapeDtypeStruct((b, s, BRANCH_WIDTH), BF16),
        scratch_shapes=[scratch] * 2,
        compiler_params=_cparams("parallel", "parallel", "arbitrary"),
        name="dil_attention",
    )(dil, dil, dil)


def _merge_kernel(x_ref, oa_ref, ob_ref, oc_ref, g_ref, wa_ref, wb_ref, wc_ref, wo_ref, out_ref):
    merged = g_ref[:, :D_MODEL].astype(F32) * jnp.dot(oa_ref[...], wa_ref[...], preferred_element_type=F32)
    merged += g_ref[:, D_MODEL:2 * D_MODEL].astype(F32) * jnp.dot(ob_ref[...], wb_ref[...], preferred_element_type=F32)
    merged += g_ref[:, 2 * D_MODEL:].astype(F32) * jnp.dot(oc_ref[...], wc_ref[...], preferred_element_type=F32)
    out_ref[...] = x_ref[...] + jnp.dot(merged.astype(BF16), wo_ref[...], preferred_element_type=F32)


def _merge(x, oa, ob, oc, gates, wa, wb, wc, wo, tm=512):
    t = x.shape[0]
    row = lambda w: pl.BlockSpec((tm, w), lambda i: (i, 0))
    full = lambda a: pl.BlockSpec(a.shape, lambda i: (0,) * a.ndim)
    return pl.pallas_call(
        _merge_kernel,
        grid=(t // tm,),
        in_specs=[row(D_MODEL), row(BRANCH_WIDTH), row(N_HEADS * LANES), row(BRANCH_WIDTH), row(GATE_COLS),
                  full(wa), full(wb), full(wc), full(wo)],
        out_specs=row(D_MODEL),
        out_shape=jax.ShapeDtypeStruct((t, D_MODEL), F32),
        compiler_params=_cparams("parallel"),
        name="merge",
    )(x, oa, ob, oc, gates, wa, wb, wc, wo)


def _route(logits_t):
    col = lambda k: logits_t[k:k + 1, :]
    gl = [col(k) for k in range(N_GROUPS)]
    gmax = functools.reduce(jnp.maximum, gl)
    taken = jnp.zeros_like(gmax) > 1.0
    is_g = []
    for k in range(N_GROUPS):
        hit = jnp.logical_and(gl[k] == gmax, jnp.logical_not(taken))
        is_g.append(hit)
        taken = jnp.logical_or(taken, hit)
    p_group = 1.0 / functools.reduce(lambda u, w: u + w, [jnp.exp(g - gmax) for g in gl])
    a = []
    for e in range(EXPERTS_PER_GROUP):
        v = col(N_GROUPS + (N_GROUPS - 1) * EXPERTS_PER_GROUP + e)
        for k in range(N_GROUPS - 2, -1, -1):
            v = jnp.where(is_g[k], col(N_GROUPS + k * EXPERTS_PER_GROUP + e), v)
        a.append(v)
    t1 = functools.reduce(jnp.maximum, a)
    taken = jnp.zeros_like(t1) > 1.0
    first = []
    for e in range(EXPERTS_PER_GROUP):
        hit = jnp.logical_and(a[e] == t1, jnp.logical_not(taken))
        first.append(hit)
        taken = jnp.logical_or(taken, hit)
    rest = [jnp.where(first[e], -jnp.inf, a[e]) for e in range(EXPERTS_PER_GROUP)]
    t2 = functools.reduce(jnp.maximum, rest)
    taken = jnp.zeros_like(t1) > 1.0
    second = []
    for e in range(EXPERTS_PER_GROUP):
        hit = jnp.logical_and(jnp.logical_and(rest[e] == t2, jnp.logical_not(first[e])), jnp.logical_not(taken))
        second.append(hit)
        taken = jnp.logical_or(taken, hit)
    d = jnp.exp(t2 - t1)
    p1 = 1.0 / (1.0 + d)
    p2 = d / (1.0 + d)
    w = [p_group * jnp.where(first[e], p1, jnp.where(second[e], p2, 0.0)) for e in range(EXPERTS_PER_GROUP)]
    return is_g, [[jnp.where(is_g[k], w[e], 0.0) for e in range(EXPERTS_PER_GROUP)] for k in range(N_GROUPS)]


def _moe_kernel(x_ref, g_ref, wr_ref, br_ref, wg_ref, wu_ref, wd_ref, gfin_ref, out_ref,
                h_ref, comb_ref, dest_row_ref, dest_col_ref, cnt_ref, acc_ref, *, final):
    gi = pl.program_id(1)
    tm = x_ref.shape[0]

    @pl.when(gi == 0)
    def _():
        x = x_ref[...]
        hf = _rms(x, g_ref[...])
        h_hi = hf.astype(BF16)
        h_lo = (hf - h_hi.astype(F32)).astype(BF16)
        hi = lax.dot_general(wr_ref[...], h_hi, _NT, preferred_element_type=F32)
        lo = lax.dot_general(wr_ref[:ROUTER_ROWS, :], h_lo, _NT, preferred_element_type=F32)
        is_g, comb = _route(hi[:ROUTER_ROWS] + hi[ROUTER_ROWS:] + lo + br_ref[...])
        sub = lax.broadcasted_iota(jnp.int32, (LANES, 1), 0)
        for k in range(N_GROUPS):
            comb_t = jnp.zeros((LANES, tm), F32)
            for e in range(EXPERTS_PER_GROUP):
                comb_t = jnp.where(sub == e, comb[k][e], comb_t)
            c = comb_t.T
            c_hi = c.astype(BF16)
            comb_ref[k] = jnp.concatenate([c_hi, (c - c_hi.astype(F32)).astype(BF16)], axis=1)
        sub8 = lax.broadcasted_iota(jnp.int32, (8, 1), 0)
        onehot = jnp.zeros((8, tm), F32)
        for k in range(N_GROUPS):
            onehot = jnp.where((sub8 == k) & is_g[k], 1.0, onehot)
        earlier = (lax.broadcasted_iota(jnp.int32, (tm, tm), 0)
                   < lax.broadcasted_iota(jnp.int32, (tm, tm), 1))
        ranks = jnp.dot(onehot.astype(BF16), jnp.where(earlier, 1.0, 0.0).astype(BF16),
                        preferred_element_type=F32)
        dest = jnp.zeros((1, tm), F32)
        for k in range(N_GROUPS):
            dest = jnp.where(is_g[k], ranks[k:k + 1, :] + float(k * tm), dest)
            cnt_ref[k] = jnp.sum(jnp.where(is_g[k], 1.0, 0.0)).astype(jnp.int32)
        dest_row_ref[...] = dest
        dest_col_ref[...] = jnp.broadcast_to(dest, (LANES, tm)).T
        h_ref[...] = h_hi
        acc_ref[...] = x

    for chunk in range(tm // MOE_SLOTS):
        @pl.when(cnt_ref[gi] > chunk * MOE_SLOTS)
        def _():
            first = (gi * tm + chunk * MOE_SLOTS).astype(F32)
            slot_r = first + lax.broadcasted_iota(jnp.int32, (MOE_SLOTS, 1), 0).astype(F32)
            pick = jnp.where(dest_row_ref[...] == slot_r, 1.0, 0.0).astype(BF16)
            hs = jnp.dot(pick, h_ref[...], preferred_element_type=F32).astype(BF16)
            cs = jnp.dot(pick, comb_ref[gi], preferred_element_type=F32)
            comb = cs[:, :LANES] + cs[:, LANES:]
            parts = []
            for e in range(EXPERTS_PER_GROUP):
                gate = jnp.dot(hs, wg_ref[e], preferred_element_type=F32)
                up = jnp.dot(hs, wu_ref[e], preferred_element_type=F32)
                parts.append((gate * (1.0 / (1.0 + jnp.exp(-gate))) * up * comb[:, e:e + 1]).astype(BF16))
            y = jnp.dot(jnp.concatenate(parts, axis=1), wd_ref[...], preferred_element_type=F32)
            slot_c = first + lax.broadcasted_iota(jnp.int32, (1, MOE_SLOTS), 1).astype(F32)
            dcol = jnp.concatenate([dest_col_ref[...]] * (MOE_SLOTS // LANES), axis=1)
            place = jnp.where(dcol == slot_c, 1.0, 0.0).astype(BF16)
            acc_ref[...] += jnp.dot(place, y.astype(BF16), preferred_element_type=F32)

    @pl.when(gi == N_GROUPS - 1)
    def _():
        y = acc_ref[...]
        out_ref[...] = _rms(y, gfin_ref[...]) if final else y


def _moe(x, g_ffn, wr, br, wg, wu, wd, g_final, final, tm=512):
    t = x.shape[0]
    row = pl.BlockSpec((tm, D_MODEL), lambda i, k: (i, 0))
    full = lambda a: pl.BlockSpec(a.shape, lambda i, k: (0,) * a.ndim)
    grp_in = pl.BlockSpec((EXPERTS_PER_GROUP, D_MODEL, D_EXPERT), lambda i, k: (k, 0, 0))
    grp_out = pl.BlockSpec((None, EXPERTS_PER_GROUP * D_EXPERT, D_MODEL), lambda i, k: (k, 0, 0))
    return pl.pallas_call(
        functools.partial(_moe_kernel, final=final),
        grid=(t // tm, N_GROUPS),
        in_specs=[row, full(g_ffn), full(wr), full(br), grp_in, grp_in, grp_out, full(g_final)],
        out_specs=row,
        out_shape=jax.ShapeDtypeStruct((t, D_MODEL), F32),
        scratch_shapes=[pltpu.VMEM((tm, D_MODEL), BF16), pltpu.VMEM((N_GROUPS, tm, 2 * LANES), BF16),
                        pltpu.VMEM((1, tm), F32), pltpu.VMEM((tm, LANES), F32), pltpu.SMEM((N_GROUPS,), jnp.int32),
                        pltpu.VMEM((tm, D_MODEL), F32)],
        compiler_params=_cparams("parallel", "arbitrary"),
        name="moe",
    )(x, g_ffn, wr, br, wg, wu, wd, g_final)


def _pad_heads(w, per_head, keep):
    lead = w.shape[:-1]
    w = w.reshape(*lead, N_HEADS, per_head)[..., keep]
    w = jnp.pad(w, [(0, 0)] * (w.ndim - 1) + [(0, LANES - w.shape[-1])])
    return w.reshape(*lead, N_HEADS * LANES)


def _prepare_weights(w_in, w_uq, w_ukv, w_branch, w_out, w_group_router, b_group_router, w_expert_router,
                     b_expert_router, w_gate, w_up, w_down):
    depth = w_in.shape[0]
    q_scale = HEAD_DIM ** -0.5
    sb_scale = jnp.concatenate([jnp.full((BRANCH_WIDTH,), q_scale, F32), jnp.ones((2 * BRANCH_WIDTH,), F32)])
    o = 0
    wsb = (w_in[:, :, o:o + SB_COLS] * sb_scale).astype(BF16)
    o += SB_COLS
    w_lat = w_in[:, :, o:o + MLA_Q_LORA + MLA_KV_LORA]
    o += MLA_Q_LORA + MLA_KV_LORA
    w_kr = w_in[:, :, o:o + MLA_ROPE]
    o += MLA_ROPE
    w_kr = jnp.pad(w_kr, ((0, 0), (0, 0), (MLA_NOPE, LANES - MLA_NOPE - MLA_ROPE)))
    wmla = jnp.concatenate([w_lat, w_kr], axis=2).astype(BF16)
    dil_scale = jnp.tile(sb_scale, N_DIL)
    wdil = (w_in[:, :, o:o + DIL_COLS] * dil_scale).astype(BF16)
    o += DIL_COLS
    wg = w_in[:, :, o:].astype(BF16)
    wuq = _pad_heads(w_uq, MLA_NOPE + MLA_ROPE, slice(None)).astype(BF16)
    wuk = _pad_heads(w_ukv, 2 * MLA_NOPE, slice(0, MLA_NOPE)).astype(BF16)
    wuv = _pad_heads(w_ukv, 2 * MLA_NOPE, slice(MLA_NOPE, None)).astype(BF16)
    wa = w_branch[:, 0].astype(BF16)
    wb = w_branch[:, 1].reshape(depth, N_HEADS, HEAD_DIM, D_MODEL)
    wb = jnp.pad(wb, ((0, 0), (0, 0), (0, LANES - HEAD_DIM), (0, 0))).reshape(depth, N_HEADS * LANES, D_MODEL)
    wb = wb.astype(BF16)
    wc = w_branch[:, 2].astype(BF16)
    wo = w_out.astype(BF16)
    n_exp = N_GROUPS * EXPERTS_PER_GROUP
    wr = jnp.concatenate([w_group_router, w_expert_router], axis=2).transpose(0, 2, 1)
    wr = jnp.pad(wr, ((0, 0), (0, ROUTER_ROWS - N_GROUPS - n_exp), (0, 0)))
    wr_hi = wr.astype(BF16)
    wr = jnp.concatenate([wr_hi, (wr - wr_hi.astype(F32)).astype(BF16)], axis=1)
    br = jnp.concatenate([b_group_router, b_expert_router], axis=1)
    br = jnp.pad(br, ((0, 0), (0, ROUTER_ROWS - N_GROUPS - n_exp)))[:, :, None]

    wgate, wup = w_gate.astype(BF16), w_up.astype(BF16)
    wdown = w_down.astype(BF16).reshape(depth, N_GROUPS, EXPERTS_PER_GROUP * D_EXPERT, D_MODEL)
    return dict(wsb=wsb, wmla=wmla, wdil=wdil, wg=wg, wuq=wuq, wuk=wuk, wuv=wuv, wa=wa, wb=wb, wc=wc, wo=wo,
                wr=wr, br=br, wgate=wgate, wup=wup, wdown=wdown)


def kernel(x, positions, g_mix, w_in, g_q_lat, w_uq, g_kv_lat, w_ukv, w_branch, w_out, g_ffn, w_group_router, b_group_router, w_expert_router, b_expert_router, w_gate, w_up, w_down, g_final):
    b, s, d = x.shape
    depth = w_in.shape[0]
    assert d == D_MODEL and w_in.shape[2] == SB_COLS + MLA_COLS + DIL_COLS + GATE_COLS
    t = b * s
    w = _prepare_weights(w_in, w_uq, w_ukv, w_branch, w_out, w_group_router, b_group_router, w_expert_router,
                         b_expert_router, w_gate, w_up, w_down)
    tabs = _rope_tables(positions.reshape(t, 1).astype(F32))
    xf = x.reshape(t, d)
    g_fin = g_final.reshape(1, d)
    for l in range(depth):
        sb, m, dil, gates = _proj(xf, b, g_mix[l][None], w["wsb"][l], w["wmla"][l], w["wdil"][l], w["wg"][l],
                                  g_q_lat[l][None], w["wuq"][l], g_kv_lat[l][None], w["wuk"][l], w["wuv"][l], tabs)
        oa = _sb_attention(sb.reshape(b, s, -1)).reshape(t, -1)
        ob = _mla_attention(m.reshape(b, s, -1)).reshape(t, -1)
        oc = _dil_attention(dil).reshape(t, -1)
        xf = _merge(xf, oa, ob, oc, gates, w["wa"][l], w["wb"][l], w["wc"][l], w["wo"][l])
        xf = _moe(xf, g_ffn[l][None], w["wr"][l], w["br"][l], w["wgate"][l], w["wup"][l],
                  w["wdown"][l], g_fin, final=(l == depth - 1))
    return xf.reshape(b, s, d)
```

```python
import functools
import math

import jax
import jax.numpy as jnp
from jax import lax
from jax.experimental import pallas as pl
from jax.experimental.pallas import tpu as pltpu

F32 = jnp.float32
BF16 = jnp.bfloat16

D_MODEL = 1024
HEAD_DIM = 64
N_HEADS = 4
BRANCH_WIDTH = N_HEADS * HEAD_DIM
ROPE_THETA = 10000.0
EPS = 1e-6
NEG = -1e30
MLA_NOPE = 64
MLA_ROPE = 32
MLA_Q_LORA = 256
MLA_KV_LORA = 128
DIL_CONFIGS = ((128, 1), (512, 4), (2048, 16))
DIL_L = 128
N_DIL = len(DIL_CONFIGS)
N_BRANCH = 3
N_GROUPS = 4
EXPERTS_PER_GROUP = 4
D_EXPERT = 256
ROUTER_ROWS = 32
SB_COLS = 3 * BRANCH_WIDTH
MLA_COLS = MLA_Q_LORA + MLA_KV_LORA + MLA_ROPE
DIL_COLS = N_DIL * 3 * BRANCH_WIDTH
GATE_COLS = N_BRANCH * D_MODEL
MLA_V = 64
LOG2_E = math.log2(math.e)

LANES = 128
VMEM_LIMIT = 56 * 1024 * 1024
DIL_SLABS = DIL_COLS // LANES
W_SHIFT = (SB_COLS + MLA_COLS) % LANES
assert SB_COLS % LANES == 0 and DIL_COLS % LANES == 0 and W_SHIFT == MLA_ROPE
DIL_CHUNK = DIL_CONFIGS[-1][0]

SB_STOP = -110.0
SB_STACKS = 2
MOE_SLOTS = 256

_NT = (((1,), (1,)), ((), ()))


def _cparams(*sem):
    return pltpu.CompilerParams(dimension_semantics=sem, vmem_limit_bytes=VMEM_LIMIT)


def _head_of_lane(width, per_head):
    return lax.broadcasted_iota(jnp.int32, (1, width), 1) // per_head


def _rope_table_kernel(pos_ref, dc_ref, dsa_ref, dsb_ref, mc_ref, msa_ref, msb_ref):
    pos = pos_ref[...]
    lane = lax.broadcasted_iota(jnp.int32, (1, LANES), 1)

    def tables(dim, in_rope, first_half, freq_idx):
        inv_freq = jnp.exp(freq_idx.astype(F32) * (-2.0 / dim * math.log(ROPE_THETA)))
        ang = pos * inv_freq
        cos, sin = jnp.cos(ang), jnp.sin(ang)
        c = jnp.where(in_rope, cos, 1.0)
        sa = jnp.where(in_rope & first_half, -sin, 0.0)
        sb = jnp.where(in_rope & jnp.logical_not(first_half), sin, 0.0)
        return c, sa, sb

    half = HEAD_DIM // 2
    c, sa, sb = tables(HEAD_DIM, lane >= 0, (lane % HEAD_DIM) < half, lane % half)
    dc_ref[...], dsa_ref[...], dsb_ref[...] = c, sa, sb
    half = MLA_ROPE // 2
    rel = lane - MLA_NOPE
    in_rope = (rel >= 0) & (rel < MLA_ROPE)
    c, sa, sb = tables(MLA_ROPE, in_rope, rel < half, jnp.where(in_rope, rel % half, 0))
    mc_ref[...], msa_ref[...], msb_ref[...] = c, sa, sb


def _rope_tables(pos_col, tm=1024):
    t = pos_col.shape[0]
    spec = pl.BlockSpec((tm, LANES), lambda i: (i, 0))
    return pl.pallas_call(
        _rope_table_kernel,
        grid=(t // tm,),
        in_specs=[pl.BlockSpec((tm, 1), lambda i: (i, 0))],
        out_specs=[spec] * 6,
        out_shape=[jax.ShapeDtypeStruct((t, LANES), F32)] * 6,
        compiler_params=_cparams("parallel"),
        name="rope_tables",
    )(pos_col)


def _rope_block(x, c, sa, sb, half):
    return x * c + pltpu.roll(x, LANES - half, 1) * sa + pltpu.roll(x, half, 1) * sb


def _rms(x, g):
    return x * lax.rsqrt(jnp.mean(x * x, axis=-1, keepdims=True) + EPS) * g


def _proj_kernel(x_ref, gmix_ref, w_ref, gq_ref, wuq_ref, gkv_ref, wuk_ref,
                 wuv_ref, dc_ref, dsa_ref, dsb_ref, mc_ref, msa_ref, msb_ref,
                 sb_ref, m_ref, dil_ref, g_ref):
    h = _rms(x_ref[...], gmix_ref[...]).astype(BF16)
    q_scale = HEAD_DIM ** -0.5

    def mm(a, w):
        return jnp.dot(a, w, preferred_element_type=F32)

    def blocks(first, last):
        return mm(h, w_ref[:, first * LANES:last * LANES])

    per = BRANCH_WIDTH // LANES
    for part in range(3):
        r = blocks(part * per, (part + 1) * per)
        sb_ref[:, part * BRANCH_WIDTH:(part + 1) * BRANCH_WIDTH] = (r * q_scale if part == 0 else r).astype(BF16)

    lat0 = SB_COLS // LANES
    lat = blocks(lat0, lat0 + 4)
    cq = lat[:, :MLA_Q_LORA]
    ckv = lat[:, MLA_Q_LORA:MLA_Q_LORA + MLA_KV_LORA]
    edge = lat[:, MLA_Q_LORA + MLA_KV_LORA:]
    mc, msa, msb = mc_ref[...], msa_ref[...], msb_ref[...]
    lane = lax.broadcasted_iota(jnp.int32, (1, LANES), 1)
    in_rope = (lane >= MLA_NOPE) & (lane < MLA_NOPE + MLA_ROPE)
    kr = jnp.where(in_rope, _rope_block(pltpu.roll(edge, MLA_NOPE, 1), mc, msa, msb, MLA_ROPE // 2), 0.0)
    q = mm(_rms(cq, gq_ref[...]).astype(BF16), wuq_ref[...])
    ckv_n = _rms(ckv, gkv_ref[...]).astype(BF16)
    kn = mm(ckv_n, wuk_ref[...])
    vv = mm(ckv_n, wuv_ref[...])
    mla_scale = (MLA_NOPE + MLA_ROPE) ** -0.5 * LOG2_E
    for hd in range(N_HEADS):
        sl = slice(hd * LANES, (hd + 1) * LANES)
        qh = _rope_block(q[:, sl], mc, msa, msb, MLA_ROPE // 2) * mla_scale
        m_ref[:, sl] = qh.astype(BF16)
        m_ref[:, N_HEADS * LANES + hd * LANES:N_HEADS * LANES + (hd + 1) * LANES] = (kn[:, sl] + kr).astype(BF16)
    lane4 = lax.broadcasted_iota(jnp.int32, (1, N_HEADS * LANES), 1)
    m_ref[:, 2 * N_HEADS * LANES:] = jnp.where(lane4 % LANES == MLA_V, 1.0, vv).astype(BF16)

    def shifted(first_block, n_out, carry):
        for j in range(0, n_out, per):
            r = blocks(first_block + j, first_block + j + per)
            for p in range(per):
                nxt = pltpu.roll(r[:, p * LANES:(p + 1) * LANES], LANES - W_SHIFT, 1)
                yield j + p, jnp.where(lane < LANES - W_SHIFT, carry, nxt)
                carry = nxt
        last.append(carry)

    dc, dsa, dsb = dc_ref[...], dsa_ref[...], dsb_ref[...]
    last = []
    for j, blk in shifted(lat0 + 4, DIL_SLABS, pltpu.roll(edge, LANES - W_SHIFT, 1)):
        part = (j // per) % 3
        if part != 2:
            blk = _rope_block(blk * q_scale if part == 0 else blk, dc, dsa, dsb, HEAD_DIM // 2)
        dil_ref[j] = blk

    for j, blk in shifted(lat0 + 4 + DIL_SLABS, GATE_COLS // LANES, last[0]):
        g_ref[:, j * LANES:(j + 1) * LANES] = (1.0 / (1.0 + jnp.exp(-blk))).astype(BF16)


def _proj(x, batch, gmix, w, gq, wuq, gkv, wuk, wuv, tabs, tm=512):
    t = x.shape[0]
    per_batch = t // batch // tm
    row = lambda w: pl.BlockSpec((tm, w), lambda i: (i, 0))
    full = lambda a: pl.BlockSpec(a.shape, lambda i: (0,) * a.ndim, pipeline_mode=pl.Buffered(1))
    weights = (gmix, w, gq, wuq, gkv, wuk, wuv)
    m_cols = 3 * N_HEADS * LANES
    dil_spec = pl.BlockSpec((None, DIL_SLABS, tm, LANES), lambda i: (i // per_batch, 0, i % per_batch, 0))
    return pl.pallas_call(
        _proj_kernel,
        grid=(t // tm,),
        in_specs=[row(D_MODEL)] + [full(w) for w in weights] + [row(LANES)] * 6,
        out_specs=[row(SB_COLS), row(m_cols), dil_spec, row(GATE_COLS)],
        out_shape=[jax.ShapeDtypeStruct((t, SB_COLS), BF16), jax.ShapeDtypeStruct((t, m_cols), BF16),
                   jax.ShapeDtypeStruct((batch, DIL_SLABS, t // batch, LANES), F32),
                   jax.ShapeDtypeStruct((t, GATE_COLS), BF16)],
        compiler_params=_cparams("parallel"),
        name="proj",
    )(x, *weights, *tabs)


def _sb_kernel(q_ref, k_ref, v_ref, o_ref, acc_ref, c_ref, *, tq):
    i = pl.program_id(1)
    q = q_ref[...]
    head = _head_of_lane(BRANCH_WIDTH, HEAD_DIM)
    per_stack = N_HEADS // SB_STACKS
    qs = [jnp.concatenate([jnp.where(head == st * per_stack + hd, q, jnp.zeros_like(q)) for hd in range(per_stack)],
                          axis=0) for st in range(SB_STACKS)]
    rows = per_stack * tq
    krow = lax.broadcasted_iota(jnp.int32, (tq, tq + LANES), 0)
    kcol = lax.broadcasted_iota(jnp.int32, (tq, tq + LANES), 1)
    later_ones = jnp.where((krow > kcol) | (kcol >= tq), 1.0, 0.0).astype(BF16)
    before = (lax.broadcasted_iota(jnp.int32, (rows, tq), 1)
              < lax.broadcasted_iota(jnp.int32, (rows, tq), 0) % tq)
    acc_ref[...] = jnp.zeros_like(acc_ref)
    c_ref[...] = jnp.zeros_like(c_ref)

    def block(j, diagonal):
        k = k_ref[pl.ds(pl.multiple_of(j * tq, tq), tq), :]
        v = v_ref[pl.ds(pl.multiple_of(j * tq, tq), tq), :]
        cmax = jnp.float32(-jnp.inf)
        for st in range(SB_STACKS):
            sl = slice(st * rows, (st + 1) * rows)
            z = lax.dot_general(qs[st], k, _NT, preferred_element_type=F32)
            sp = jnp.log(1.0 + jnp.exp(-jnp.abs(z)))
            log_beta = jnp.minimum(z, 0.0) - sp
            log_1m = log_beta - z
            if diagonal:
                log_1m = jnp.where(before, log_1m, 0.0)
            sums = jnp.dot(log_1m.astype(BF16), later_ones, preferred_element_type=F32)
            c = c_ref[sl, :]
            w = jnp.exp(log_beta + jnp.concatenate([c] * (tq // LANES), axis=1) + sums[:, :tq])
            if diagonal:
                w = jnp.where(before, w, 0.0)
            acc_ref[sl, :] += jnp.dot(w.astype(BF16), v, preferred_element_type=F32)
            c_new = c + sums[:, tq:]
            c_ref[sl, :] = c_new
            cmax = jnp.maximum(cmax, jnp.max(c_new))
        return cmax

    first = block(i, True)
    lax.while_loop(lambda cr: (cr[0] >= 0) & (cr[1] > SB_STOP), lambda cr: (cr[0] - 1, block(cr[0], False)),
                   (i - 1, first))
    out = jnp.zeros((tq, BRANCH_WIDTH), F32)
    for hd in range(N_HEADS):
        out = jnp.where(head == hd, acc_ref[hd * tq:(hd + 1) * tq, :], out)
    o_ref[...] = out.astype(BF16)


def _sb_attention(sb, tq=256):
    b, s, _ = sb.shape
    return pl.pallas_call(
        functools.partial(_sb_kernel, tq=tq),
        grid=(b, s // tq),
        in_specs=[pl.BlockSpec((None, tq, BRANCH_WIDTH), lambda bi, i: (bi, i, 0)),
                  pl.BlockSpec((None, s, BRANCH_WIDTH), lambda bi, i: (bi, 0, 1)),
                  pl.BlockSpec((None, s, BRANCH_WIDTH), lambda bi, i: (bi, 0, 2))],
        out_specs=pl.BlockSpec((None, tq, BRANCH_WIDTH), lambda bi, i: (bi, i, 0)),
        out_shape=jax.ShapeDtypeStruct((b, s, BRANCH_WIDTH), BF16),
        scratch_shapes=[pltpu.VMEM((N_HEADS * tq, BRANCH_WIDTH), F32), pltpu.VMEM((N_HEADS * tq, LANES), F32)],
        compiler_params=_cparams("parallel", "parallel"),
        name="sb_attention",
    )(sb, sb, sb)


def _mla_kernel(q_ref, k_ref, v_ref, o_ref, m_ref, acc_ref, *, t):
    i = pl.program_id(1)
    row = lax.broadcasted_iota(jnp.int32, (t, t), 0)
    col = lax.broadcasted_iota(jnp.int32, (t, t), 1)
    m_ref[...] = jnp.full(m_ref.shape, NEG, F32)
    acc_ref[...] = jnp.zeros_like(acc_ref)

    def step(j, masked):
        rows = pl.ds(pl.multiple_of(j * t, t), t)
        for hd in range(N_HEADS):
            sl = slice(hd * LANES, (hd + 1) * LANES)
            s = lax.dot_general(q_ref[:, sl], k_ref[rows, sl], _NT, preferred_element_type=F32)
            if masked:
                s = jnp.where(col <= row, s, NEG)
            m_prev = m_ref[hd]
            m_new = jnp.maximum(m_prev, jnp.max(s, axis=1, keepdims=True))
            p = jnp.exp2((s - jnp.concatenate([m_new] * (t // LANES), axis=1)).astype(BF16))
            acc_ref[hd] = jnp.exp2(m_prev - m_new) * acc_ref[hd] + jnp.dot(p, v_ref[rows, sl],
                                                                           preferred_element_type=F32)
            m_ref[hd] = m_new

    def pair(j, carry):
        step(2 * j, False)
        step(2 * j + 1, False)
        return carry

    lax.fori_loop(0, i // 2, pair, 0)
    pl.when(i % 2 == 1)(lambda: step(i - 1, False))
    step(i, True)
    for hd in range(N_HEADS):
        acc = acc_ref[hd]
        o_ref[:, hd * LANES:(hd + 1) * LANES] = (acc / acc[:, MLA_V:MLA_V + 1]).astype(BF16)


def _mla_attention(m, t=512):
    b, s, _ = m.shape
    w = N_HEADS * LANES
    t = min(t, s)
    return pl.pallas_call(
        functools.partial(_mla_kernel, t=t),
        grid=(b, s // t),
        in_specs=[pl.BlockSpec((None, t, w), lambda bi, i: (bi, i, 0)),
                  pl.BlockSpec((None, s, w), lambda bi, i: (bi, 0, 1)),
                  pl.BlockSpec((None, s, w), lambda bi, i: (bi, 0, 2))],
        out_specs=pl.BlockSpec((None, t, w), lambda bi, i: (bi, i, 0)),
        out_shape=jax.ShapeDtypeStruct((b, s, w), BF16),
        scratch_shapes=[pltpu.VMEM((N_HEADS, t, LANES), F32), pltpu.VMEM((N_HEADS, t, LANES), F32)],
        compiler_params=_cparams("parallel", "parallel"),
        name="mla_attention",
    )(m, m, m)


def _small_unroll(n):
    return next((u for u in (4, 5, 3, 2) if n % u == 0), 1)


def _dil_kernel(cur_ref, kp_ref, vp_ref, out_ref, og_ref, lg_ref):
    g = pl.program_id(2)
    has_prev_chunk = pl.program_id(1) > 0
    L, C = DIL_L, DIL_CHUNK
    halves = BRANCH_WIDTH // LANES
    head = _head_of_lane(BRANCH_WIDTH, HEAD_DIM)
    a = lax.broadcasted_iota(jnp.int32, (N_HEADS * L, 2 * L), 0) % L
    c = lax.broadcasted_iota(jnp.int32, (N_HEADS * L, 2 * L), 1)
    band = (c >= a) & (c <= a + L)
    band_first = band & ((c >= L) | has_prev_chunk)

    def rows_of(start, d):
        return pl.ds(start, L) if d == 1 else pl.ds(start, L, stride=d)

    def load(ref, slab, start, d):
        idx = rows_of(start, d)
        return jnp.concatenate([ref[slab + p, idx, :] for p in range(halves)], axis=1).astype(BF16)

    def unit(gi, d, q_start, kprev, vprev, valid):
        q = load(cur_ref, 0, q_start, d)
        kw = jnp.concatenate([kprev, load(cur_ref, halves, q_start, d)], axis=0)
        vw = jnp.concatenate([vprev, load(cur_ref, 2 * halves, q_start, d)], axis=0)
        qs = jnp.concatenate([jnp.where(head == hd, q, jnp.zeros_like(q)) for hd in range(N_HEADS)], axis=0)
        sc = lax.dot_general(qs, kw, _NT, preferred_element_type=F32)
        sc = jnp.where(valid, sc, NEG)
        mx = jnp.max(sc, axis=1, keepdims=True)
        p = jnp.exp(sc - mx)
        sm = jnp.sum(p, axis=1, keepdims=True)
        oh = jnp.dot((p * (1.0 / sm)).astype(BF16), vw, preferred_element_type=F32)
        lse = mx + jnp.log(sm)
        o = jnp.zeros((L, BRANCH_WIDTH), F32)
        lw = jnp.zeros((L, BRANCH_WIDTH), F32)
        for hd in range(N_HEADS):
            o = jnp.where(head == hd, oh[hd * L:(hd + 1) * L], o)
            lw = jnp.where(head == hd, lse[hd * L:(hd + 1) * L], lw)
        idx = rows_of(q_start, d)
        for p_ in range(halves):
            og_ref[gi, p_, idx, :] = o[:, p_ * LANES:(p_ + 1) * LANES]
            lg_ref[gi, p_, idx, :] = lw[:, p_ * LANES:(p_ + 1) * LANES]

    def group(gi, d):
        nb = C // (L * d)

        def first(r, carry):
            start = (nb - 1) * L * d + r
            unit(gi, d, r, load(kp_ref, 0, start, d), load(vp_ref, 0, start, d), band_first)
            return carry

        lax.fori_loop(0, d, first, 0, unroll=_small_unroll(d))

        def rest(idx, carry):
            start = (1 + idx // d) * (L * d) + idx % d
            prev = start - L * d
            unit(gi, d, start, load(cur_ref, halves, prev, d), load(cur_ref, 2 * halves, prev, d), band)
            return carry

        if nb > 1:
            lax.fori_loop(0, d * (nb - 1), rest, 0, unroll=_small_unroll(d * (nb - 1)))

    for gi, (_, d) in enumerate(DIL_CONFIGS):
        pl.when(g == gi)(functools.partial(group, gi, d))

    @pl.when(g == N_DIL - 1)
    def _():
        for p_ in range(halves):
            lgs = [lg_ref[gi, p_] for gi in range(N_DIL)]
            mx = functools.reduce(jnp.maximum, lgs)
            es = [jnp.exp(l - mx) for l in lgs]
            num = functools.reduce(lambda u, w: u + w, [es[gi] * og_ref[gi, p_] for gi in range(N_DIL)])
            den = functools.reduce(lambda u, w: u + w, es)
            out_ref[:, p_ * LANES:(p_ + 1) * LANES] = (num / den).astype(BF16)


def _dil_attention(dil):
    b, slabs, s, _ = dil.shape
    C = DIL_CHUNK
    assert slabs == DIL_SLABS and s % C == 0 and all(w // d == DIL_L and C % w == 0 for w, d in DIL_CONFIGS)
    per_group = DIL_SLABS // N_DIL
    halves = BRANCH_WIDTH // LANES

    def prev(t):
        return pl.BlockSpec((None, halves, C, LANES),
                            lambda bi, ch, g: (bi, 3 * g + t, jnp.maximum(ch - 1, 0), 0))

    scratch = pltpu.VMEM((N_DIL, halves, C, LANES), F32)
    return pl.pallas_call(
        _dil_kernel,
        grid=(b, s // C, N_DIL),
        in_specs=[pl.BlockSpec((None, per_group, C, LANES), lambda bi, ch, g: (bi, g, ch, 0)), prev(1), prev(2)],
        out_specs=pl.BlockSpec((None, C, BRANCH_WIDTH), lambda bi, ch, g: (bi, ch, 0)),
        out_shape=jax.ShapeDtypeStruct((b, s, BRANCH_WIDTH), BF16),
        scratch_shapes=[scratch] * 2,
        compiler_params=_cparams("parallel", "parallel", "arbitrary"),
        name="dil_attention",
    )(dil, dil, dil)


def _merge_kernel(x_ref, oa_ref, ob_ref, oc_ref, g_ref, wa_ref, wb_ref, wc_ref, wo_ref, out_ref):
    merged = g_ref[:, :D_MODEL].astype(F32) * jnp.dot(oa_ref[...], wa_ref[...], preferred_element_type=F32)
    merged += g_ref[:, D_MODEL:2 * D_MODEL].astype(F32) * jnp.dot(ob_ref[...], wb_ref[...], preferred_element_type=F32)
    merged += g_ref[:, 2 * D_MODEL:].astype(F32) * jnp.dot(oc_ref[...], wc_ref[...], preferred_element_type=F32)
    out_ref[...] = x_ref[...] + jnp.dot(merged.astype(BF16), wo_ref[...], preferred_element_type=F32)


def _merge(x, oa, ob, oc, gates, wa, wb, wc, wo, tm=512):
    t = x.shape[0]
    row = lambda w: pl.BlockSpec((tm, w), lambda i: (i, 0))
    full = lambda a: pl.BlockSpec(a.shape, lambda i: (0,) * a.ndim)
    return pl.pallas_call(
        _merge_kernel,
        grid=(t // tm,),
        in_specs=[row(D_MODEL), row(BRANCH_WIDTH), row(N_HEADS * LANES), row(BRANCH_WIDTH), row(GATE_COLS),
                  full(wa), full(wb), full(wc), full(wo)],
        out_specs=row(D_MODEL),
        out_shape=jax.ShapeDtypeStruct((t, D_MODEL), F32),
        compiler_params=_cparams("parallel"),
        name="merge",
    )(x, oa, ob, oc, gates, wa, wb, wc, wo)


def _route(logits_t):
    col = lambda k: logits_t[k:k + 1, :]
    gl = [col(k) for k in range(N_GROUPS)]
    gmax = functools.reduce(jnp.maximum, gl)
    taken = jnp.zeros_like(gmax) > 1.0
    is_g = []
    for k in range(N_GROUPS):
        hit = jnp.logical_and(gl[k] == gmax, jnp.logical_not(taken))
        is_g.append(hit)
        taken = jnp.logical_or(taken, hit)
    p_group = 1.0 / functools.reduce(lambda u, w: u + w, [jnp.exp(g - gmax) for g in gl])
    a = []
    for e in range(EXPERTS_PER_GROUP):
        v = col(N_GROUPS + (N_GROUPS - 1) * EXPERTS_PER_GROUP + e)
        for k in range(N_GROUPS - 2, -1, -1):
            v = jnp.where(is_g[k], col(N_GROUPS + k * EXPERTS_PER_GROUP + e), v)
        a.append(v)
    t1 = functools.reduce(jnp.maximum, a)
    taken = jnp.zeros_like(t1) > 1.0
    first = []
    for e in range(EXPERTS_PER_GROUP):
        hit = jnp.logical_and(a[e] == t1, jnp.logical_not(taken))
        first.append(hit)
        taken = jnp.logical_or(taken, hit)
    rest = [jnp.where(first[e], -jnp.inf, a[e]) for e in range(EXPERTS_PER_GROUP)]
    t2 = functools.reduce(jnp.maximum, rest)
    taken = jnp.zeros_like(t1) > 1.0
    second = []
    for e in range(EXPERTS_PER_GROUP):
        hit = jnp.logical_and(jnp.logical_and(rest[e] == t2, jnp.logical_not(first[e])), jnp.logical_not(taken))
        second.append(hit)
        taken = jnp.logical_or(taken, hit)
    d = jnp.exp(t2 - t1)
    p1 = 1.0 / (1.0 + d)
    p2 = d / (1.0 + d)
    w = [p_group * jnp.where(first[e], p1, jnp.where(second[e], p2, 0.0)) for e in range(EXPERTS_PER_GROUP)]
    return is_g, [[jnp.where(is_g[k], w[e], 0.0) for e in range(EXPERTS_PER_GROUP)] for k in range(N_GROUPS)]


def _moe_kernel(x_ref, g_ref, wr_ref, br_ref, wg_ref, wu_ref, wd_ref, gfin_ref, out_ref,
                h_ref, comb_ref, dest_row_ref, dest_col_ref, cnt_ref, acc_ref, *, final):
    gi = pl.program_id(1)
    tm = x_ref.shape[0]

    @pl.when(gi == 0)
    def _():
        x = x_ref[...]
        hf = _rms(x, g_ref[...])
        h_hi = hf.astype(BF16)
        h_lo = (hf - h_hi.astype(F32)).astype(BF16)
        hi = lax.dot_general(wr_ref[...], h_hi, _NT, preferred_element_type=F32)
        lo = lax.dot_general(wr_ref[:ROUTER_ROWS, :], h_lo, _NT, preferred_element_type=F32)
        is_g, comb = _route(hi[:ROUTER_ROWS] + hi[ROUTER_ROWS:] + lo + br_ref[...])
        sub = lax.broadcasted_iota(jnp.int32, (LANES, 1), 0)
        for k in range(N_GROUPS):
            comb_t = jnp.zeros((LANES, tm), F32)
            for e in range(EXPERTS_PER_GROUP):
                comb_t = jnp.where(sub == e, comb[k][e], comb_t)
            c = comb_t.T
            c_hi = c.astype(BF16)
            comb_ref[k] = jnp.concatenate([c_hi, (c - c_hi.astype(F32)).astype(BF16)], axis=1)
        sub8 = lax.broadcasted_iota(jnp.int32, (8, 1), 0)
        onehot = jnp.zeros((8, tm), F32)
        for k in range(N_GROUPS):
            onehot = jnp.where((sub8 == k) & is_g[k], 1.0, onehot)
        earlier = (lax.broadcasted_iota(jnp.int32, (tm, tm), 0)
                   < lax.broadcasted_iota(jnp.int32, (tm, tm), 1))
        ranks = jnp.dot(onehot.astype(BF16), jnp.where(earlier, 1.0, 0.0).astype(BF16),
                        preferred_element_type=F32)
        dest = jnp.zeros((1, tm), F32)
        for k in range(N_GROUPS):
            dest = jnp.where(is_g[k], ranks[k:k + 1, :] + float(k * tm), dest)
            cnt_ref[k] = jnp.sum(jnp.where(is_g[k], 1.0, 0.0)).astype(jnp.int32)
        dest_row_ref[...] = dest
        dest_col_ref[...] = jnp.broadcast_to(dest, (LANES, tm)).T
        h_ref[...] = h_hi
        acc_ref[...] = x

    for chunk in range(tm // MOE_SLOTS):
        @pl.when(cnt_ref[gi] > chunk * MOE_SLOTS)
        def _():
            first = (gi * tm + chunk * MOE_SLOTS).astype(F32)
            slot_r = first + lax.broadcasted_iota(jnp.int32, (MOE_SLOTS, 1), 0).astype(F32)
            pick = jnp.where(dest_row_ref[...] == slot_r, 1.0, 0.0).astype(BF16)
            hs = jnp.dot(pick, h_ref[...], preferred_element_type=F32).astype(BF16)
            cs = jnp.dot(pick, comb_ref[gi], preferred_element_type=F32)
            comb = cs[:, :LANES] + cs[:, LANES:]
            parts = []
            for e in range(EXPERTS_PER_GROUP):
                gate = jnp.dot(hs, wg_ref[e], preferred_element_type=F32)
                up = jnp.dot(hs, wu_ref[e], preferred_element_type=F32)
                parts.append((gate * (1.0 / (1.0 + jnp.exp(-gate))) * up * comb[:, e:e + 1]).astype(BF16))
            y = jnp.dot(jnp.concatenate(parts, axis=1), wd_ref[...], preferred_element_type=F32)
            slot_c = first + lax.broadcasted_iota(jnp.int32, (1, MOE_SLOTS), 1).astype(F32)
            dcol = jnp.concatenate([dest_col_ref[...]] * (MOE_SLOTS // LANES), axis=1)
            place = jnp.where(dcol == slot_c, 1.0, 0.0).astype(BF16)
            acc_ref[...] += jnp.dot(place, y.astype(BF16), preferred_element_type=F32)

    @pl.when(gi == N_GROUPS - 1)
    def _():
        y = acc_ref[...]
        out_ref[...] = _rms(y, gfin_ref[...]) if final else y


def _moe(x, g_ffn, wr, br, wg, wu, wd, g_final, final, tm=1024):
    t = x.shape[0]
    row = pl.BlockSpec((tm, D_MODEL), lambda i, k: (i, 0))
    full = lambda a: pl.BlockSpec(a.shape, lambda i, k: (0,) * a.ndim)
    grp_in = pl.BlockSpec((EXPERTS_PER_GROUP, D_MODEL, D_EXPERT), lambda i, k: (k, 0, 0))
    grp_out = pl.BlockSpec((None, EXPERTS_PER_GROUP * D_EXPERT, D_MODEL), lambda i, k: (k, 0, 0))
    return pl.pallas_call(
        functools.partial(_moe_kernel, final=final),
        grid=(t // tm, N_GROUPS),
        in_specs=[row, full(g_ffn), full(wr), full(br), grp_in, grp_in, grp_out, full(g_final)],
        out_specs=row,
        out_shape=jax.ShapeDtypeStruct((t, D_MODEL), F32),
        scratch_shapes=[pltpu.VMEM((tm, D_MODEL), BF16), pltpu.VMEM((N_GROUPS, tm, 2 * LANES), BF16),
                        pltpu.VMEM((1, tm), F32), pltpu.VMEM((tm, LANES), F32), pltpu.SMEM((N_GROUPS,), jnp.int32),
                        pltpu.VMEM((tm, D_MODEL), F32)],
        compiler_params=_cparams("parallel", "arbitrary"),
        name="moe",
    )(x, g_ffn, wr, br, wg, wu, wd, g_final)


def _pad_heads(w, per_head, keep):
    lead = w.shape[:-1]
    w = w.reshape(*lead, N_HEADS, per_head)[..., keep]
    w = jnp.pad(w, [(0, 0)] * (w.ndim - 1) + [(0, LANES - w.shape[-1])])
    return w.reshape(*lead, N_HEADS * LANES)


def _prepare_weights(w_in, w_uq, w_ukv, w_branch, w_out, w_group_router, b_group_router, w_expert_router,
                     b_expert_router, w_gate, w_up, w_down):
    depth = w_in.shape[0]
    w = jnp.pad(w_in.astype(BF16), ((0, 0), (0, 0), (0, -w_in.shape[2] % LANES)))
    wuq = _pad_heads(w_uq, MLA_NOPE + MLA_ROPE, slice(None)).astype(BF16)
    wuk = _pad_heads(w_ukv, 2 * MLA_NOPE, slice(0, MLA_NOPE)).astype(BF16)
    wuv = _pad_heads(w_ukv, 2 * MLA_NOPE, slice(MLA_NOPE, None)).astype(BF16)
    wa = w_branch[:, 0].astype(BF16)
    wb = w_branch[:, 1].reshape(depth, N_HEADS, HEAD_DIM, D_MODEL)
    wb = jnp.pad(wb, ((0, 0), (0, 0), (0, LANES - HEAD_DIM), (0, 0))).reshape(depth, N_HEADS * LANES, D_MODEL)
    wb = wb.astype(BF16)
    wc = w_branch[:, 2].astype(BF16)
    wo = w_out.astype(BF16)
    n_exp = N_GROUPS * EXPERTS_PER_GROUP
    wr = jnp.concatenate([w_group_router, w_expert_router], axis=2).transpose(0, 2, 1)
    wr = jnp.pad(wr, ((0, 0), (0, ROUTER_ROWS - N_GROUPS - n_exp), (0, 0)))
    wr_hi = wr.astype(BF16)
    wr = jnp.concatenate([wr_hi, (wr - wr_hi.astype(F32)).astype(BF16)], axis=1)
    br = jnp.concatenate([b_group_router, b_expert_router], axis=1)
    br = jnp.pad(br, ((0, 0), (0, ROUTER_ROWS - N_GROUPS - n_exp)))[:, :, None]

    wgate, wup = w_gate.astype(BF16), w_up.astype(BF16)
    wdown = w_down.astype(BF16).reshape(depth, N_GROUPS, EXPERTS_PER_GROUP * D_EXPERT, D_MODEL)
    return dict(w=w, wuq=wuq, wuk=wuk, wuv=wuv, wa=wa, wb=wb, wc=wc, wo=wo,
                wr=wr, br=br, wgate=wgate, wup=wup, wdown=wdown)


def kernel(x, positions, g_mix, w_in, g_q_lat, w_uq, g_kv_lat, w_ukv, w_branch, w_out, g_ffn, w_group_router, b_group_router, w_expert_router, b_expert_router, w_gate, w_up, w_down, g_final):
    b, s, d = x.shape
    depth = w_in.shape[0]
    assert d == D_MODEL and w_in.shape[2] == SB_COLS + MLA_COLS + DIL_COLS + GATE_COLS
    t = b * s
    w = _prepare_weights(w_in, w_uq, w_ukv, w_branch, w_out, w_group_router, b_group_router, w_expert_router,
                         b_expert_router, w_gate, w_up, w_down)
    tabs = _rope_tables(positions.reshape(t, 1).astype(F32))
    xf = x.reshape(t, d)
    g_fin = g_final.reshape(1, d)
    for l in range(depth):
        sb, m, dil, gates = _proj(xf, b, g_mix[l][None], w["w"][l], g_q_lat[l][None], w["wuq"][l],
                                  g_kv_lat[l][None], w["wuk"][l], w["wuv"][l], tabs)
        oa = _sb_attention(sb.reshape(b, s, -1)).reshape(t, -1)
        ob = _mla_attention(m.reshape(b, s, -1)).reshape(t, -1)
        oc = _dil_attention(dil).reshape(t, -1)
        xf = _merge(xf, oa, ob, oc, gates, w["wa"][l], w["wb"][l], w["wc"][l], w["wo"][l])
        xf = _moe(xf, g_ffn[l][None], w["wr"][l], w["br"][l], w["wgate"][l], w["wup"][l],
                  w["wdown"][l], g_fin, final=(l == depth - 1))
    return xf.reshape(b, s, d)
```

```python
import functools
import math

import jax
import jax.numpy as jnp
from jax import lax
from jax.experimental import pallas as pl
from jax.experimental.pallas import tpu as pltpu

F32 = jnp.float32
BF16 = jnp.bfloat16

D_MODEL = 1024
HEAD_DIM = 64
N_HEADS = 4
BRANCH_WIDTH = N_HEADS * HEAD_DIM
ROPE_THETA = 10000.0
EPS = 1e-6
NEG = -1e30
MLA_NOPE = 64
MLA_ROPE = 32
MLA_Q_LORA = 256
MLA_KV_LORA = 128
DIL_CONFIGS = ((128, 1), (512, 4), (2048, 16))
DIL_L = 128
N_DIL = len(DIL_CONFIGS)
N_BRANCH = 3
N_GROUPS = 4
EXPERTS_PER_GROUP = 4
D_EXPERT = 256
ROUTER_ROWS = 32
SB_COLS = 3 * BRANCH_WIDTH
MLA_COLS = MLA_Q_LORA + MLA_KV_LORA + MLA_ROPE
DIL_COLS = N_DIL * 3 * BRANCH_WIDTH
GATE_COLS = N_BRANCH * D_MODEL
MLA_V = 64
LOG2_E = math.log2(math.e)

LANES = 128
VMEM_LIMIT = 56 * 1024 * 1024
DIL_SLABS = DIL_COLS // LANES
W_SHIFT = (SB_COLS + MLA_COLS) % LANES
assert SB_COLS % LANES == 0 and DIL_COLS % LANES == 0 and W_SHIFT == MLA_ROPE
DIL_CHUNK = DIL_CONFIGS[-1][0]

SB_STOP = -110.0
SB_STACKS = 2
MLA_SPAN = 4
MOE_SLOTS = 256

_NT = (((1,), (1,)), ((), ()))


def _cparams(*sem):
    return pltpu.CompilerParams(dimension_semantics=sem, vmem_limit_bytes=VMEM_LIMIT)


def _head_of_lane(width, per_head):
    return lax.broadcasted_iota(jnp.int32, (1, width), 1) // per_head


def _rope_table_kernel(pos_ref, dc_ref, dsa_ref, dsb_ref, mc_ref, msa_ref, msb_ref):
    pos = pos_ref[...]
    lane = lax.broadcasted_iota(jnp.int32, (1, LANES), 1)

    def tables(dim, in_rope, first_half, freq_idx):
        inv_freq = jnp.exp(freq_idx.astype(F32) * (-2.0 / dim * math.log(ROPE_THETA)))
        ang = pos * inv_freq
        cos, sin = jnp.cos(ang), jnp.sin(ang)
        c = jnp.where(in_rope, cos, 1.0)
        sa = jnp.where(in_rope & first_half, -sin, 0.0)
        sb = jnp.where(in_rope & jnp.logical_not(first_half), sin, 0.0)
        return c, sa, sb

    half = HEAD_DIM // 2
    c, sa, sb = tables(HEAD_DIM, lane >= 0, (lane % HEAD_DIM) < half, lane % half)
    dc_ref[...], dsa_ref[...], dsb_ref[...] = c, sa, sb
    half = MLA_ROPE // 2
    rel = lane - MLA_NOPE
    in_rope = (rel >= 0) & (rel < MLA_ROPE)
    c, sa, sb = tables(MLA_ROPE, in_rope, rel < half, jnp.where(in_rope, rel % half, 0))
    mc_ref[...], msa_ref[...], msb_ref[...] = c, sa, sb


def _rope_tables(pos_col, tm=1024):
    t = pos_col.shape[0]
    spec = pl.BlockSpec((tm, LANES), lambda i: (i, 0))
    return pl.pallas_call(
        _rope_table_kernel,
        grid=(t // tm,),
        in_specs=[pl.BlockSpec((tm, 1), lambda i: (i, 0))],
        out_specs=[spec] * 6,
        out_shape=[jax.ShapeDtypeStruct((t, LANES), F32)] * 6,
        compiler_params=_cparams("parallel"),
        name="rope_tables",
    )(pos_col)


def _rope_block(x, c, sa, sb, half):
    return x * c + pltpu.roll(x, LANES - half, 1) * sa + pltpu.roll(x, half, 1) * sb


def _rms(x, g):
    return x * lax.rsqrt(jnp.mean(x * x, axis=-1, keepdims=True) + EPS) * g


def _proj_kernel(x_ref, gmix_ref, w_ref, gq_ref, wuq_ref, gkv_ref, wuk_ref,
                 wuv_ref, dc_ref, dsa_ref, dsb_ref, mc_ref, msa_ref, msb_ref,
                 sb_ref, m_ref, dil_ref, g_ref):
    h = _rms(x_ref[...], gmix_ref[...]).astype(BF16)
    q_scale = HEAD_DIM ** -0.5

    def mm(a, w):
        return jnp.dot(a, w, preferred_element_type=F32)

    def blocks(first, last):
        return mm(h, w_ref[:, first * LANES:last * LANES])

    per = BRANCH_WIDTH // LANES
    for part in range(3):
        r = blocks(part * per, (part + 1) * per)
        sb_ref[:, part * BRANCH_WIDTH:(part + 1) * BRANCH_WIDTH] = (r * q_scale if part == 0 else r).astype(BF16)

    lat0 = SB_COLS // LANES
    lat = blocks(lat0, lat0 + 4)
    cq = lat[:, :MLA_Q_LORA]
    ckv = lat[:, MLA_Q_LORA:MLA_Q_LORA + MLA_KV_LORA]
    edge = lat[:, MLA_Q_LORA + MLA_KV_LORA:]
    mc, msa, msb = mc_ref[...], msa_ref[...], msb_ref[...]
    lane = lax.broadcasted_iota(jnp.int32, (1, LANES), 1)
    in_rope = (lane >= MLA_NOPE) & (lane < MLA_NOPE + MLA_ROPE)
    kr = jnp.where(in_rope, _rope_block(pltpu.roll(edge, MLA_NOPE, 1), mc, msa, msb, MLA_ROPE // 2), 0.0)
    q = mm(_rms(cq, gq_ref[...]).astype(BF16), wuq_ref[...])
    ckv_n = _rms(ckv, gkv_ref[...]).astype(BF16)
    kn = mm(ckv_n, wuk_ref[...])
    vv = mm(ckv_n, wuv_ref[...])
    mla_scale = (MLA_NOPE + MLA_ROPE) ** -0.5 * LOG2_E
    for hd in range(N_HEADS):
        sl = slice(hd * LANES, (hd + 1) * LANES)
        qh = _rope_block(q[:, sl], mc, msa, msb, MLA_ROPE // 2) * mla_scale
        m_ref[:, sl] = qh.astype(BF16)
        m_ref[:, N_HEADS * LANES + hd * LANES:N_HEADS * LANES + (hd + 1) * LANES] = (kn[:, sl] + kr).astype(BF16)
    lane4 = lax.broadcasted_iota(jnp.int32, (1, N_HEADS * LANES), 1)
    m_ref[:, 2 * N_HEADS * LANES:] = jnp.where(lane4 % LANES == MLA_V, 1.0, vv).astype(BF16)

    def shifted(first_block, n_out, carry):
        for j in range(0, n_out, per):
            r = blocks(first_block + j, first_block + j + per)
            for p in range(per):
                nxt = pltpu.roll(r[:, p * LANES:(p + 1) * LANES], LANES - W_SHIFT, 1)
                yield j + p, jnp.where(lane < LANES - W_SHIFT, carry, nxt)
                carry = nxt
        last.append(carry)

    dc, dsa, dsb = dc_ref[...], dsa_ref[...], dsb_ref[...]
    last = []
    for j, blk in shifted(lat0 + 4, DIL_SLABS, pltpu.roll(edge, LANES - W_SHIFT, 1)):
        part = (j // per) % 3
        if part != 2:
            blk = _rope_block(blk * q_scale if part == 0 else blk, dc, dsa, dsb, HEAD_DIM // 2)
        dil_ref[j] = blk

    for j, blk in shifted(lat0 + 4 + DIL_SLABS, GATE_COLS // LANES, last[0]):
        g_ref[:, j * LANES:(j + 1) * LANES] = (1.0 / (1.0 + jnp.exp(-blk))).astype(BF16)


def _layer_spec(a, layer, **kw):
    return pl.BlockSpec((None,) + a.shape[1:], lambda *_: (layer,) + (0,) * (a.ndim - 1), **kw)


def _proj(x, batch, layer, gmix, w, gq, wuq, gkv, wuk, wuv, tabs, tm=512):
    t = x.shape[0]
    per_batch = t // batch // tm
    row = lambda w: pl.BlockSpec((tm, w), lambda i: (i, 0))
    full = lambda a: _layer_spec(a, layer, pipeline_mode=pl.Buffered(1))
    weights = (gmix, w, gq, wuq, gkv, wuk, wuv)
    m_cols = 3 * N_HEADS * LANES
    dil_spec = pl.BlockSpec((None, DIL_SLABS, tm, LANES), lambda i: (i // per_batch, 0, i % per_batch, 0))
    return pl.pallas_call(
        _proj_kernel,
        grid=(t // tm,),
        in_specs=[row(D_MODEL)] + [full(w) for w in weights] + [row(LANES)] * 6,
        out_specs=[row(SB_COLS), row(m_cols), dil_spec, row(GATE_COLS)],
        out_shape=[jax.ShapeDtypeStruct((t, SB_COLS), BF16), jax.ShapeDtypeStruct((t, m_cols), BF16),
                   jax.ShapeDtypeStruct((batch, DIL_SLABS, t // batch, LANES), F32),
                   jax.ShapeDtypeStruct((t, GATE_COLS), BF16)],
        compiler_params=_cparams("parallel"),
        name="proj",
    )(x, *weights, *tabs)


def _sb_kernel(q_ref, k_ref, v_ref, o_ref, acc_ref, c_ref, *, tq):
    i = pl.program_id(1)
    q = q_ref[...]
    head = _head_of_lane(BRANCH_WIDTH, HEAD_DIM)
    per_stack = N_HEADS // SB_STACKS
    qs = [jnp.concatenate([jnp.where(head == st * per_stack + hd, q, jnp.zeros_like(q)) for hd in range(per_stack)],
                          axis=0) for st in range(SB_STACKS)]
    rows = per_stack * tq
    krow = lax.broadcasted_iota(jnp.int32, (tq, tq + LANES), 0)
    kcol = lax.broadcasted_iota(jnp.int32, (tq, tq + LANES), 1)
    later_ones = jnp.where((krow > kcol) | (kcol >= tq), 1.0, 0.0).astype(BF16)
    before = (lax.broadcasted_iota(jnp.int32, (rows, tq), 1)
              < lax.broadcasted_iota(jnp.int32, (rows, tq), 0) % tq)
    acc_ref[...] = jnp.zeros_like(acc_ref)
    c_ref[...] = jnp.zeros_like(c_ref)

    def block(j, diagonal):
        k = k_ref[pl.ds(pl.multiple_of(j * tq, tq), tq), :]
        v = v_ref[pl.ds(pl.multiple_of(j * tq, tq), tq), :]
        cmax = jnp.float32(-jnp.inf)
        for st in range(SB_STACKS):
            sl = slice(st * rows, (st + 1) * rows)
            z = lax.dot_general(qs[st], k, _NT, preferred_element_type=F32)
            sp = jnp.log(1.0 + jnp.exp(-jnp.abs(z)))
            log_beta = jnp.minimum(z, 0.0) - sp
            log_1m = log_beta - z
            if diagonal:
                log_1m = jnp.where(before, log_1m, 0.0)
            sums = jnp.dot(log_1m.astype(BF16), later_ones, preferred_element_type=F32)
            c = c_ref[sl, :]
            w = jnp.exp(log_beta + jnp.concatenate([c] * (tq // LANES), axis=1) + sums[:, :tq])
            if diagonal:
                w = jnp.where(before, w, 0.0)
            acc_ref[sl, :] += jnp.dot(w.astype(BF16), v, preferred_element_type=F32)
            c_new = c + sums[:, tq:]
            c_ref[sl, :] = c_new
            cmax = jnp.maximum(cmax, jnp.max(c_new))
        return cmax

    first = block(i, True)
    lax.while_loop(lambda cr: (cr[0] >= 0) & (cr[1] > SB_STOP), lambda cr: (cr[0] - 1, block(cr[0], False)),
                   (i - 1, first))
    out = jnp.zeros((tq, BRANCH_WIDTH), F32)
    for hd in range(N_HEADS):
        out = jnp.where(head == hd, acc_ref[hd * tq:(hd + 1) * tq, :], out)
    o_ref[...] = out.astype(BF16)


def _sb_attention(sb, tq=256):
    b, s, _ = sb.shape
    return pl.pallas_call(
        functools.partial(_sb_kernel, tq=tq),
        grid=(b, s // tq),
        in_specs=[pl.BlockSpec((None, tq, BRANCH_WIDTH), lambda bi, i: (bi, i, 0)),
                  pl.BlockSpec((None, s, BRANCH_WIDTH), lambda bi, i: (bi, 0, 1)),
                  pl.BlockSpec((None, s, BRANCH_WIDTH), lambda bi, i: (bi, 0, 2))],
        out_specs=pl.BlockSpec((None, tq, BRANCH_WIDTH), lambda bi, i: (bi, i, 0)),
        out_shape=jax.ShapeDtypeStruct((b, s, BRANCH_WIDTH), BF16),
        scratch_shapes=[pltpu.VMEM((N_HEADS * tq, BRANCH_WIDTH), F32), pltpu.VMEM((N_HEADS * tq, LANES), F32)],
        compiler_params=_cparams("parallel", "parallel"),
        name="sb_attention",
    )(sb, sb, sb)


def _mla_kernel(q_ref, k_ref, v_ref, o_ref, m_ref, acc_ref, *, t):
    i = pl.program_id(1)
    row = lax.broadcasted_iota(jnp.int32, (t, t), 0)
    col = lax.broadcasted_iota(jnp.int32, (t, t), 1)
    m_ref[...] = jnp.full(m_ref.shape, NEG, F32)
    acc_ref[...] = jnp.zeros_like(acc_ref)

    def step(j, masked, span=1):
        rows = pl.ds(pl.multiple_of(j * t, t), span * t)
        for hd in range(N_HEADS):
            sl = slice(hd * LANES, (hd + 1) * LANES)
            s = lax.dot_general(q_ref[:, sl], k_ref[rows, sl], _NT, preferred_element_type=F32)
            if masked:
                s = jnp.where(col <= row, s, NEG)
            m_prev = m_ref[hd]
            m_new = jnp.maximum(m_prev, jnp.max(s, axis=1, keepdims=True))
            p = jnp.exp2((s - jnp.concatenate([m_new] * (span * t // LANES), axis=1)).astype(BF16))
            acc_ref[hd] = jnp.exp2(m_prev - m_new) * acc_ref[hd] + jnp.dot(p, v_ref[rows, sl],
                                                                           preferred_element_type=F32)
            m_ref[hd] = m_new

    def wide(j, carry):
        step(MLA_SPAN * j, False, span=MLA_SPAN)
        return carry

    lax.fori_loop(0, i // MLA_SPAN, wide, 0)
    done = (i // MLA_SPAN) * MLA_SPAN
    span = MLA_SPAN // 2
    while span >= 1:
        pl.when((i - done) % (2 * span) >= span)(functools.partial(step, done, False, span))
        done = done + jnp.where((i - done) % (2 * span) >= span, span, 0)
        span //= 2
    step(i, True)
    for hd in range(N_HEADS):
        acc = acc_ref[hd]
        o_ref[:, hd * LANES:(hd + 1) * LANES] = (acc / acc[:, MLA_V:MLA_V + 1]).astype(BF16)


def _mla_attention(m, t=512):
    b, s, _ = m.shape
    w = N_HEADS * LANES
    t = min(t, s)
    return pl.pallas_call(
        functools.partial(_mla_kernel, t=t),
        grid=(b, s // t),
        in_specs=[pl.BlockSpec((None, t, w), lambda bi, i: (bi, i, 0)),
                  pl.BlockSpec((None, s, w), lambda bi, i: (bi, 0, 1)),
                  pl.BlockSpec((None, s, w), lambda bi, i: (bi, 0, 2))],
        out_specs=pl.BlockSpec((None, t, w), lambda bi, i: (bi, i, 0)),
        out_shape=jax.ShapeDtypeStruct((b, s, w), BF16),
        scratch_shapes=[pltpu.VMEM((N_HEADS, t, LANES), F32), pltpu.VMEM((N_HEADS, t, LANES), F32)],
        compiler_params=_cparams("parallel", "parallel"),
        name="mla_attention",
    )(m, m, m)


def _small_unroll(n):
    return next((u for u in (4, 5, 3, 2) if n % u == 0), 1)


def _dil_kernel(cur_ref, kp_ref, vp_ref, out_ref, og_ref, lg_ref):
    g = pl.program_id(2)
    has_prev_chunk = pl.program_id(1) > 0
    L, C = DIL_L, DIL_CHUNK
    halves = BRANCH_WIDTH // LANES
    head = _head_of_lane(BRANCH_WIDTH, HEAD_DIM)
    a = lax.broadcasted_iota(jnp.int32, (N_HEADS * L, 2 * L), 0) % L
    c = lax.broadcasted_iota(jnp.int32, (N_HEADS * L, 2 * L), 1)
    band = (c >= a) & (c <= a + L)
    band_first = band & ((c >= L) | has_prev_chunk)

    def rows_of(start, d):
        return pl.ds(start, L) if d == 1 else pl.ds(start, L, stride=d)

    def load(ref, slab, start, d):
        idx = rows_of(start, d)
        return jnp.concatenate([ref[slab + p, idx, :] for p in range(halves)], axis=1).astype(BF16)

    def unit(gi, d, q_start, kprev, vprev, valid):
        q = load(cur_ref, 0, q_start, d)
        kw = jnp.concatenate([kprev, load(cur_ref, halves, q_start, d)], axis=0)
        vw = jnp.concatenate([vprev, load(cur_ref, 2 * halves, q_start, d)], axis=0)
        qs = jnp.concatenate([jnp.where(head == hd, q, jnp.zeros_like(q)) for hd in range(N_HEADS)], axis=0)
        sc = lax.dot_general(qs, kw, _NT, preferred_element_type=F32)
        sc = jnp.where(valid, sc, NEG)
        mx = jnp.max(sc, axis=1, keepdims=True)
        p = jnp.exp(sc - mx)
        sm = jnp.sum(p, axis=1, keepdims=True)
        oh = jnp.dot((p * (1.0 / sm)).astype(BF16), vw, preferred_element_type=F32)
        lse = mx + jnp.log(sm)
        o = jnp.zeros((L, BRANCH_WIDTH), F32)
        lw = jnp.zeros((L, BRANCH_WIDTH), F32)
        for hd in range(N_HEADS):
            o = jnp.where(head == hd, oh[hd * L:(hd + 1) * L], o)
            lw = jnp.where(head == hd, lse[hd * L:(hd + 1) * L], lw)
        idx = rows_of(q_start, d)
        for p_ in range(halves):
            og_ref[gi, p_, idx, :] = o[:, p_ * LANES:(p_ + 1) * LANES]
            lg_ref[gi, p_, idx, :] = lw[:, p_ * LANES:(p_ + 1) * LANES]

    def group(gi, d):
        nb = C // (L * d)

        def first(r, carry):
            start = (nb - 1) * L * d + r
            unit(gi, d, r, load(kp_ref, 0, start, d), load(vp_ref, 0, start, d), band_first)
            return carry

        lax.fori_loop(0, d, first, 0, unroll=_small_unroll(d))

        def rest(idx, carry):
            start = (1 + idx // d) * (L * d) + idx % d
            prev = start - L * d
            unit(gi, d, start, load(cur_ref, halves, prev, d), load(cur_ref, 2 * halves, prev, d), band)
            return carry

        if nb > 1:
            lax.fori_loop(0, d * (nb - 1), rest, 0, unroll=_small_unroll(d * (nb - 1)))

    for gi, (_, d) in enumerate(DIL_CONFIGS):
        pl.when(g == gi)(functools.partial(group, gi, d))

    @pl.when(g == N_DIL - 1)
    def _():
        for p_ in range(halves):
            lgs = [lg_ref[gi, p_] for gi in range(N_DIL)]
            mx = functools.reduce(jnp.maximum, lgs)
            es = [jnp.exp(l - mx) for l in lgs]
            num = functools.reduce(lambda u, w: u + w, [es[gi] * og_ref[gi, p_] for gi in range(N_DIL)])
            den = functools.reduce(lambda u, w: u + w, es)
            out_ref[:, p_ * LANES:(p_ + 1) * LANES] = (num / den).astype(BF16)


def _dil_attention(dil):
    b, slabs, s, _ = dil.shape
    C = DIL_CHUNK
    assert slabs == DIL_SLABS and s % C == 0 and all(w // d == DIL_L and C % w == 0 for w, d in DIL_CONFIGS)
    per_group = DIL_SLABS // N_DIL
    halves = BRANCH_WIDTH // LANES

    def prev(t):
        return pl.BlockSpec((None, halves, C, LANES),
                            lambda bi, ch, g: (bi, 3 * g + t, jnp.maximum(ch - 1, 0), 0))

    scratch = pltpu.VMEM((N_DIL, halves, C, LANES), F32)
    return pl.pallas_call(
        _dil_kernel,
        grid=(b, s // C, N_DIL),
        in_specs=[pl.BlockSpec((None, per_group, C, LANES), lambda bi, ch, g: (bi, g, ch, 0)), prev(1), prev(2)],
        out_specs=pl.BlockSpec((None, C, BRANCH_WIDTH), lambda bi, ch, g: (bi, ch, 0)),
        out_shape=jax.ShapeDtypeStruct((b, s, BRANCH_WIDTH), BF16),
        scratch_shapes=[scratch] * 2,
        compiler_params=_cparams("parallel", "parallel", "arbitrary"),
        name="dil_attention",
    )(dil, dil, dil)


def _merge_kernel(x_ref, oa_ref, ob_ref, oc_ref, g_ref, wa_ref, wb_ref, wc_ref, wo_ref, out_ref):
    merged = g_ref[:, :D_MODEL].astype(F32) * jnp.dot(oa_ref[...], wa_ref[...], preferred_element_type=F32)
    merged += g_ref[:, D_MODEL:2 * D_MODEL].astype(F32) * jnp.dot(ob_ref[...], wb_ref[...], preferred_element_type=F32)
    merged += g_ref[:, 2 * D_MODEL:].astype(F32) * jnp.dot(oc_ref[...], wc_ref[...], preferred_element_type=F32)
    out_ref[...] = x_ref[...] + jnp.dot(merged.astype(BF16), wo_ref[...], preferred_element_type=F32)


def _merge(x, oa, ob, oc, gates, layer, wa, wb, wc, wo, tm=512):
    t = x.shape[0]
    row = lambda w: pl.BlockSpec((tm, w), lambda i: (i, 0))
    full = lambda a: _layer_spec(a, layer)
    return pl.pallas_call(
        _merge_kernel,
        grid=(t // tm,),
        in_specs=[row(D_MODEL), row(BRANCH_WIDTH), row(N_HEADS * LANES), row(BRANCH_WIDTH), row(GATE_COLS),
                  full(wa), full(wb), full(wc), full(wo)],
        out_specs=row(D_MODEL),
        out_shape=jax.ShapeDtypeStruct((t, D_MODEL), F32),
        compiler_params=_cparams("parallel"),
        name="merge",
    )(x, oa, ob, oc, gates, wa, wb, wc, wo)


def _route(logits_t):
    col = lambda k: logits_t[k:k + 1, :]
    gl = [col(k) for k in range(N_GROUPS)]
    gmax = functools.reduce(jnp.maximum, gl)
    taken = jnp.zeros_like(gmax) > 1.0
    is_g = []
    for k in range(N_GROUPS):
        hit = jnp.logical_and(gl[k] == gmax, jnp.logical_not(taken))
        is_g.append(hit)
        taken = jnp.logical_or(taken, hit)
    p_group = 1.0 / functools.reduce(lambda u, w: u + w, [jnp.exp(g - gmax) for g in gl])
    a = []
    for e in range(EXPERTS_PER_GROUP):
        v = col(N_GROUPS + (N_GROUPS - 1) * EXPERTS_PER_GROUP + e)
        for k in range(N_GROUPS - 2, -1, -1):
            v = jnp.where(is_g[k], col(N_GROUPS + k * EXPERTS_PER_GROUP + e), v)
        a.append(v)
    t1 = functools.reduce(jnp.maximum, a)
    taken = jnp.zeros_like(t1) > 1.0
    first = []
    for e in range(EXPERTS_PER_GROUP):
        hit = jnp.logical_and(a[e] == t1, jnp.logical_not(taken))
        first.append(hit)
        taken = jnp.logical_or(taken, hit)
    rest = [jnp.where(first[e], -jnp.inf, a[e]) for e in range(EXPERTS_PER_GROUP)]
    t2 = functools.reduce(jnp.maximum, rest)
    taken = jnp.zeros_like(t1) > 1.0
    second = []
    for e in range(EXPERTS_PER_GROUP):
        hit = jnp.logical_and(jnp.logical_and(rest[e] == t2, jnp.logical_not(first[e])), jnp.logical_not(taken))
        second.append(hit)
        taken = jnp.logical_or(taken, hit)
    d = jnp.exp(t2 - t1)
    p1 = 1.0 / (1.0 + d)
    p2 = d / (1.0 + d)
    w = [p_group * jnp.where(first[e], p1, jnp.where(second[e], p2, 0.0)) for e in range(EXPERTS_PER_GROUP)]
    return is_g, [[jnp.where(is_g[k], w[e], 0.0) for e in range(EXPERTS_PER_GROUP)] for k in range(N_GROUPS)]


def _moe_kernel(x_ref, g_ref, wr_ref, br_ref, wg_ref, wu_ref, wd_ref, gfin_ref, out_ref,
                h_ref, comb_ref, dest_row_ref, dest_col_ref, cnt_ref, acc_ref, *, final):
    gi = pl.program_id(1)
    tm = x_ref.shape[0]

    @pl.when(gi == 0)
    def _():
        x = x_ref[...]
        hf = _rms(x, g_ref[...])
        h_hi = hf.astype(BF16)
        h_lo = (hf - h_hi.astype(F32)).astype(BF16)
        hi = lax.dot_general(wr_ref[...], h_hi, _NT, preferred_element_type=F32)
        lo = lax.dot_general(wr_ref[:ROUTER_ROWS, :], h_lo, _NT, preferred_element_type=F32)
        is_g, comb = _route(hi[:ROUTER_ROWS] + hi[ROUTER_ROWS:] + lo + br_ref[...])
        sub = lax.broadcasted_iota(jnp.int32, (LANES, 1), 0)
        for k in range(N_GROUPS):
            comb_t = jnp.zeros((LANES, tm), F32)
            for e in range(EXPERTS_PER_GROUP):
                comb_t = jnp.where(sub == e, comb[k][e], comb_t)
            c = comb_t.T
            c_hi = c.astype(BF16)
            comb_ref[k] = jnp.concatenate([c_hi, (c - c_hi.astype(F32)).astype(BF16)], axis=1)
        sub8 = lax.broadcasted_iota(jnp.int32, (8, 1), 0)
        onehot = jnp.zeros((8, tm), F32)
        for k in range(N_GROUPS):
            onehot = jnp.where((sub8 == k) & is_g[k], 1.0, onehot)
        earlier = (lax.broadcasted_iota(jnp.int32, (tm, tm), 0)
                   < lax.broadcasted_iota(jnp.int32, (tm, tm), 1))
        ranks = jnp.dot(onehot.astype(BF16), jnp.where(earlier, 1.0, 0.0).astype(BF16),
                        preferred_element_type=F32)
        dest = jnp.zeros((1, tm), F32)
        for k in range(N_GROUPS):
            dest = jnp.where(is_g[k], ranks[k:k + 1, :] + float(k * tm), dest)
            cnt_ref[k] = jnp.sum(jnp.where(is_g[k], 1.0, 0.0)).astype(jnp.int32)
        dest_row_ref[...] = dest
        dest_col_ref[...] = jnp.broadcast_to(dest, (LANES, tm)).T
        h_ref[...] = h_hi
        acc_ref[...] = x

    for chunk in range(tm // MOE_SLOTS):
        @pl.when(cnt_ref[gi] > chunk * MOE_SLOTS)
        def _():
            first = (gi * tm + chunk * MOE_SLOTS).astype(F32)
            slot_r = first + lax.broadcasted_iota(jnp.int32, (MOE_SLOTS, 1), 0).astype(F32)
            pick = jnp.where(dest_row_ref[...] == slot_r, 1.0, 0.0).astype(BF16)
            hs = jnp.dot(pick, h_ref[...], preferred_element_type=F32).astype(BF16)
            cs = jnp.dot(pick, comb_ref[gi], preferred_element_type=F32)
            comb = cs[:, :LANES] + cs[:, LANES:]
            parts = []
            for e in range(EXPERTS_PER_GROUP):
                gate = jnp.dot(hs, wg_ref[e], preferred_element_type=F32)
                up = jnp.dot(hs, wu_ref[e], preferred_element_type=F32)
                parts.append((gate * (1.0 / (1.0 + jnp.exp(-gate))) * up * comb[:, e:e + 1]).astype(BF16))
            y = jnp.dot(jnp.concatenate(parts, axis=1), wd_ref[...], preferred_element_type=F32)
            slot_c = first + lax.broadcasted_iota(jnp.int32, (1, MOE_SLOTS), 1).astype(F32)
            dcol = jnp.concatenate([dest_col_ref[...]] * (MOE_SLOTS // LANES), axis=1)
            place = jnp.where(dcol == slot_c, 1.0, 0.0).astype(BF16)
            acc_ref[...] += jnp.dot(place, y.astype(BF16), preferred_element_type=F32)

    @pl.when(gi == N_GROUPS - 1)
    def _():
        y = acc_ref[...]
        out_ref[...] = _rms(y, gfin_ref[...]) if final else y


def _moe(x, layer, g_ffn, wr, br, wg, wu, wd, g_final, final, tm=1024):
    t = x.shape[0]
    row = pl.BlockSpec((tm, D_MODEL), lambda i, k: (i, 0))
    full = lambda a: _layer_spec(a, layer)
    grp_in = pl.BlockSpec((EXPERTS_PER_GROUP, D_MODEL, D_EXPERT), lambda i, k: (layer * N_GROUPS + k, 0, 0))
    grp_out = pl.BlockSpec((None, EXPERTS_PER_GROUP * D_EXPERT, D_MODEL), lambda i, k: (layer * N_GROUPS + k, 0, 0))
    return pl.pallas_call(
        functools.partial(_moe_kernel, final=final),
        grid=(t // tm, N_GROUPS),
        in_specs=[row, full(g_ffn), full(wr), full(br), grp_in, grp_in, grp_out,
                  pl.BlockSpec(g_final.shape, lambda i, k: (0, 0))],
        out_specs=row,
        out_shape=jax.ShapeDtypeStruct((t, D_MODEL), F32),
        scratch_shapes=[pltpu.VMEM((tm, D_MODEL), BF16), pltpu.VMEM((N_GROUPS, tm, 2 * LANES), BF16),
                        pltpu.VMEM((1, tm), F32), pltpu.VMEM((tm, LANES), F32), pltpu.SMEM((N_GROUPS,), jnp.int32),
                        pltpu.VMEM((tm, D_MODEL), F32)],
        compiler_params=_cparams("parallel", "arbitrary"),
        name="moe",
    )(x, g_ffn, wr, br, wg, wu, wd, g_final)


def _pad_heads(w, per_head, keep):
    lead = w.shape[:-1]
    w = w.reshape(*lead, N_HEADS, per_head)[..., keep]
    w = jnp.pad(w, [(0, 0)] * (w.ndim - 1) + [(0, LANES - w.shape[-1])])
    return w.reshape(*lead, N_HEADS * LANES)


def _prepare_weights(w_in, w_uq, w_ukv, w_branch, w_out, w_group_router, b_group_router, w_expert_router,
                     b_expert_router, w_gate, w_up, w_down):
    depth = w_in.shape[0]
    w = jnp.pad(w_in.astype(BF16), ((0, 0), (0, 0), (0, -w_in.shape[2] % LANES)))
    wuq = _pad_heads(w_uq, MLA_NOPE + MLA_ROPE, slice(None)).astype(BF16)
    wuk = _pad_heads(w_ukv, 2 * MLA_NOPE, slice(0, MLA_NOPE)).astype(BF16)
    wuv = _pad_heads(w_ukv, 2 * MLA_NOPE, slice(MLA_NOPE, None)).astype(BF16)
    wa = w_branch[:, 0].astype(BF16)
    wb = w_branch[:, 1].reshape(depth, N_HEADS, HEAD_DIM, D_MODEL)
    wb = jnp.pad(wb, ((0, 0), (0, 0), (0, LANES - HEAD_DIM), (0, 0))).reshape(depth, N_HEADS * LANES, D_MODEL)
    wb = wb.astype(BF16)
    wc = w_branch[:, 2].astype(BF16)
    wo = w_out.astype(BF16)
    n_exp = N_GROUPS * EXPERTS_PER_GROUP
    wr = jnp.concatenate([w_group_router, w_expert_router], axis=2).transpose(0, 2, 1)
    wr = jnp.pad(wr, ((0, 0), (0, ROUTER_ROWS - N_GROUPS - n_exp), (0, 0)))
    wr_hi = wr.astype(BF16)
    wr = jnp.concatenate([wr_hi, (wr - wr_hi.astype(F32)).astype(BF16)], axis=1)
    br = jnp.concatenate([b_group_router, b_expert_router], axis=1)
    br = jnp.pad(br, ((0, 0), (0, ROUTER_ROWS - N_GROUPS - n_exp)))[:, :, None]

    wgate = w_gate.astype(BF16).reshape(depth * n_exp, D_MODEL, D_EXPERT)
    wup = w_up.astype(BF16).reshape(depth * n_exp, D_MODEL, D_EXPERT)
    wdown = w_down.astype(BF16).reshape(depth * N_GROUPS, EXPERTS_PER_GROUP * D_EXPERT, D_MODEL)
    return dict(w=w, wuq=wuq, wuk=wuk, wuv=wuv, wa=wa, wb=wb, wc=wc, wo=wo,
                wr=wr, br=br, wgate=wgate, wup=wup, wdown=wdown)


def kernel(x, positions, g_mix, w_in, g_q_lat, w_uq, g_kv_lat, w_ukv, w_branch, w_out, g_ffn, w_group_router, b_group_router, w_expert_router, b_expert_router, w_gate, w_up, w_down, g_final):
    b, s, d = x.shape
    depth = w_in.shape[0]
    assert d == D_MODEL and w_in.shape[2] == SB_COLS + MLA_COLS + DIL_COLS + GATE_COLS
    t = b * s
    w = _prepare_weights(w_in, w_uq, w_ukv, w_branch, w_out, w_group_router, b_group_router, w_expert_router,
                         b_expert_router, w_gate, w_up, w_down)
    tabs = _rope_tables(positions.reshape(t, 1).astype(F32))
    xf = x.reshape(t, d)
    g_fin = g_final.reshape(1, d)
    gains = [g[:, None, :] for g in (g_mix, g_q_lat, g_kv_lat, g_ffn)]
    for l in range(depth):
        sb, m, dil, gates = _proj(xf, b, l, gains[0], w["w"], gains[1], w["wuq"], gains[2], w["wuk"], w["wuv"], tabs)
        oa = _sb_attention(sb.reshape(b, s, -1)).reshape(t, -1)
        ob = _mla_attention(m.reshape(b, s, -1)).reshape(t, -1)
        oc = _dil_attention(dil).reshape(t, -1)
        xf = _merge(xf, oa, ob, oc, gates, l, w["wa"], w["wb"], w["wc"], w["wo"])
        xf = _moe(xf, l, gains[3], w["wr"], w["br"], w["wgate"], w["wup"], w["wdown"], g_fin,
                  final=(l == depth - 1))
    return xf.reshape(b, s, d)
```

```python
import functools
import math

import jax
import jax.numpy as jnp
from jax import lax
from jax.experimental import pallas as pl
from jax.experimental.pallas import tpu as pltpu

F32 = jnp.float32
BF16 = jnp.bfloat16

D_MODEL = 1024
HEAD_DIM = 64
N_HEADS = 4
BRANCH_WIDTH = N_HEADS * HEAD_DIM
ROPE_THETA = 10000.0
EPS = 1e-6
NEG = -1e30
MLA_NOPE = 64
MLA_ROPE = 32
MLA_Q_LORA = 256
MLA_KV_LORA = 128
DIL_CONFIGS = ((128, 1), (512, 4), (2048, 16))
DIL_L = 128
N_DIL = len(DIL_CONFIGS)
N_BRANCH = 3
N_GROUPS = 4
EXPERTS_PER_GROUP = 4
D_EXPERT = 256
ROUTER_ROWS = 32
SB_COLS = 3 * BRANCH_WIDTH
MLA_COLS = MLA_Q_LORA + MLA_KV_LORA + MLA_ROPE
DIL_COLS = N_DIL * 3 * BRANCH_WIDTH
GATE_COLS = N_BRANCH * D_MODEL
MLA_V = 64
LOG2_E = math.log2(math.e)

LANES = 128
VMEM_LIMIT = 56 * 1024 * 1024
DIL_SLABS = DIL_COLS // LANES
W_SHIFT = (SB_COLS + MLA_COLS) % LANES
assert SB_COLS % LANES == 0 and DIL_COLS % LANES == 0 and W_SHIFT == MLA_ROPE
DIL_CHUNK = DIL_CONFIGS[-1][0]

SB_STOP = -110.0
SB_STACKS = 2
MLA_SPAN = 4
MOE_SLOTS = 256

_NT = (((1,), (1,)), ((), ()))


def _cparams(*sem):
    return pltpu.CompilerParams(dimension_semantics=sem, vmem_limit_bytes=VMEM_LIMIT)


def _head_of_lane(width, per_head):
    return lax.broadcasted_iota(jnp.int32, (1, width), 1) // per_head


def _rope_table_kernel(pos_ref, dc_ref, dsa_ref, dsb_ref, mc_ref, msa_ref, msb_ref):
    pos = pos_ref[...]
    lane = lax.broadcasted_iota(jnp.int32, (1, LANES), 1)
    dh, mh = HEAD_DIM // 2, MLA_ROPE // 2
    is_d, is_m = lane < dh, (lane >= dh) & (lane < dh + mh)
    expo = jnp.where(is_d, lane.astype(F32) * (-2.0 / HEAD_DIM), (lane - dh).astype(F32) * (-2.0 / MLA_ROPE))
    ang = pos * jnp.exp(expo * math.log(ROPE_THETA))
    cos, sin = jnp.cos(ang), jnp.sin(ang)

    def spread(x, keep, shifts):
        x = jnp.where(keep, x, 0.0)
        return functools.reduce(lambda u, w: u + w, [pltpu.roll(x, s, 1) if s else x for s in shifts])

    d_shifts = tuple(range(0, LANES, dh))
    dcos, dsin = spread(cos, is_d, d_shifts), spread(sin, is_d, d_shifts)
    first = (lane % HEAD_DIM) < dh
    dc_ref[...], dsa_ref[...], dsb_ref[...] = dcos, jnp.where(first, -dsin, 0.0), jnp.where(first, 0.0, dsin)
    rel = lane - MLA_NOPE
    in_rope = (rel >= 0) & (rel < MLA_ROPE)
    m_shifts = (MLA_NOPE - dh, MLA_NOPE - dh + mh)
    mcos, msin = spread(cos, is_m, m_shifts), spread(sin, is_m, m_shifts)
    mc_ref[...] = jnp.where(in_rope, mcos, 1.0)
    msa_ref[...] = jnp.where(in_rope & (rel < mh), -msin, 0.0)
    msb_ref[...] = jnp.where(in_rope & (rel >= mh), msin, 0.0)


def _rope_tables(pos_col, tm=1024):
    t = pos_col.shape[0]
    spec = pl.BlockSpec((tm, LANES), lambda i: (i, 0))
    return pl.pallas_call(
        _rope_table_kernel,
        grid=(t // tm,),
        in_specs=[pl.BlockSpec((tm, 1), lambda i: (i, 0))],
        out_specs=[spec] * 6,
        out_shape=[jax.ShapeDtypeStruct((t, LANES), F32)] * 6,
        compiler_params=_cparams("parallel"),
        name="rope_tables",
    )(pos_col)


def _rope_block(x, c, sa, sb, half):
    return x * c + pltpu.roll(x, LANES - half, 1) * sa + pltpu.roll(x, half, 1) * sb


def _rms(x, g):
    return x * lax.rsqrt(jnp.mean(x * x, axis=-1, keepdims=True) + EPS) * g


def _shift_roll(block):
    return pltpu.roll(block, LANES - W_SHIFT, 1)


def _shifted(blocks, first_block, n_out, carry, per, last):
    lane = lax.broadcasted_iota(jnp.int32, (1, LANES), 1)
    for j in range(0, n_out, per):
        r = blocks(first_block + j, first_block + j + per)
        for p in range(per):
            nxt = _shift_roll(r[:, p * LANES:(p + 1) * LANES])
            yield j + p, jnp.where(lane < LANES - W_SHIFT, carry, nxt)
            carry = nxt
    last.append(carry)


def _proj_kernel(x_ref, gmix_ref, w_ref, gq_ref, wuq_ref, gkv_ref, wuk_ref,
                 wuv_ref, dc_ref, dsa_ref, dsb_ref, mc_ref, msa_ref, msb_ref,
                 sb_ref, m_ref, dil_ref, g_ref):
    h = _rms(x_ref[...], gmix_ref[...]).astype(BF16)
    q_scale = HEAD_DIM ** -0.5

    def mm(a, w):
        return jnp.dot(a, w, preferred_element_type=F32)

    def blocks(first, last):
        return mm(h, w_ref[:, first * LANES:last * LANES])

    per = BRANCH_WIDTH // LANES
    for part in range(3):
        r = blocks(part * per, (part + 1) * per)
        sb_ref[:, part * BRANCH_WIDTH:(part + 1) * BRANCH_WIDTH] = (r * q_scale if part == 0 else r).astype(BF16)

    lat0 = SB_COLS // LANES
    lat = blocks(lat0, lat0 + 4)
    cq = lat[:, :MLA_Q_LORA]
    ckv = lat[:, MLA_Q_LORA:MLA_Q_LORA + MLA_KV_LORA]
    edge = lat[:, MLA_Q_LORA + MLA_KV_LORA:]
    mc, msa, msb = mc_ref[...], msa_ref[...], msb_ref[...]
    lane = lax.broadcasted_iota(jnp.int32, (1, LANES), 1)
    in_rope = (lane >= MLA_NOPE) & (lane < MLA_NOPE + MLA_ROPE)
    kr = jnp.where(in_rope, _rope_block(pltpu.roll(edge, MLA_NOPE, 1), mc, msa, msb, MLA_ROPE // 2), 0.0)
    q = mm(_rms(cq, gq_ref[...]).astype(BF16), wuq_ref[...])
    ckv_n = _rms(ckv, gkv_ref[...]).astype(BF16)
    kn = mm(ckv_n, wuk_ref[...])
    vv = mm(ckv_n, wuv_ref[...])
    mla_scale = (MLA_NOPE + MLA_ROPE) ** -0.5 * LOG2_E
    for hd in range(N_HEADS):
        sl = slice(hd * LANES, (hd + 1) * LANES)
        qh = _rope_block(q[:, sl], mc, msa, msb, MLA_ROPE // 2) * mla_scale
        m_ref[:, sl] = qh.astype(BF16)
        m_ref[:, N_HEADS * LANES + hd * LANES:N_HEADS * LANES + (hd + 1) * LANES] = (kn[:, sl] + kr).astype(BF16)
    lane4 = lax.broadcasted_iota(jnp.int32, (1, N_HEADS * LANES), 1)
    m_ref[:, 2 * N_HEADS * LANES:] = jnp.where(lane4 % LANES == MLA_V, 1.0, vv).astype(BF16)

    dc, dsa, dsb = dc_ref[...], dsa_ref[...], dsb_ref[...]
    last = []
    for j, blk in _shifted(blocks, lat0 + 4, DIL_SLABS, _shift_roll(edge), per, last):
        part = (j // per) % 3
        if part != 2:
            blk = _rope_block(blk * q_scale if part == 0 else blk, dc, dsa, dsb, HEAD_DIM // 2)
        dil_ref[j] = blk

    for j, blk in _shifted(blocks, lat0 + 4 + DIL_SLABS, GATE_COLS // LANES, last[0], per, last):
        g_ref[:, j * LANES:(j + 1) * LANES] = (1.0 / (1.0 + jnp.exp(-blk))).astype(BF16)


def _layer_spec(a, layer, **kw):
    return pl.BlockSpec((None,) + a.shape[1:], lambda *_: (layer,) + (0,) * (a.ndim - 1), **kw)


def _proj(x, batch, layer, gmix, w, gq, wuq, gkv, wuk, wuv, tabs, tm=512):
    t = x.shape[0]
    per_batch = t // batch // tm
    row = lambda w: pl.BlockSpec((tm, w), lambda i: (i, 0))
    full = lambda a: _layer_spec(a, layer, pipeline_mode=pl.Buffered(1))
    weights = (gmix, w, gq, wuq, gkv, wuk, wuv)
    m_cols = 3 * N_HEADS * LANES
    dil_spec = pl.BlockSpec((None, DIL_SLABS, tm, LANES), lambda i: (i // per_batch, 0, i % per_batch, 0))
    return pl.pallas_call(
        _proj_kernel,
        grid=(t // tm,),
        in_specs=[row(D_MODEL)] + [full(a) for a in weights] + [row(LANES)] * 6,
        out_specs=[row(SB_COLS), row(m_cols), dil_spec, row(GATE_COLS)],
        out_shape=[jax.ShapeDtypeStruct((t, SB_COLS), BF16), jax.ShapeDtypeStruct((t, m_cols), BF16),
                   jax.ShapeDtypeStruct((batch, DIL_SLABS, t // batch, LANES), F32),
                   jax.ShapeDtypeStruct((t, GATE_COLS), BF16)],
        compiler_params=_cparams("parallel"),
        name="proj",
    )(x, *weights, *tabs)


def _sb_kernel(q_ref, k_ref, v_ref, o_ref, acc_ref, c_ref, *, tq):
    i = pl.program_id(1)
    q = q_ref[...]
    head = _head_of_lane(BRANCH_WIDTH, HEAD_DIM)
    per_stack = N_HEADS // SB_STACKS
    qs = [jnp.concatenate([jnp.where(head == st * per_stack + hd, q, jnp.zeros_like(q)) for hd in range(per_stack)],
                          axis=0) for st in range(SB_STACKS)]
    rows = per_stack * tq
    krow = lax.broadcasted_iota(jnp.int32, (tq, tq + LANES), 0)
    kcol = lax.broadcasted_iota(jnp.int32, (tq, tq + LANES), 1)
    later_ones = jnp.where((krow > kcol) | (kcol >= tq), 1.0, 0.0).astype(BF16)
    before = (lax.broadcasted_iota(jnp.int32, (rows, tq), 1)
              < lax.broadcasted_iota(jnp.int32, (rows, tq), 0) % tq)
    acc_ref[...] = jnp.zeros_like(acc_ref)
    c_ref[...] = jnp.zeros_like(c_ref)

    def block(j, diagonal):
        k = k_ref[pl.ds(pl.multiple_of(j * tq, tq), tq), :]
        v = v_ref[pl.ds(pl.multiple_of(j * tq, tq), tq), :]
        cmax = jnp.float32(-jnp.inf)
        for st in range(SB_STACKS):
            sl = slice(st * rows, (st + 1) * rows)
            z = lax.dot_general(qs[st], k, _NT, preferred_element_type=F32)
            sp = jnp.log(1.0 + jnp.exp(-jnp.abs(z)))
            log_beta = jnp.minimum(z, 0.0) - sp
            log_1m = log_beta - z
            if diagonal:
                log_1m = jnp.where(before, log_1m, 0.0)
            sums = jnp.dot(log_1m.astype(BF16), later_ones, preferred_element_type=F32)
            c = c_ref[sl, :]
            w = jnp.exp(log_beta + jnp.concatenate([c] * (tq // LANES), axis=1) + sums[:, :tq])
            if diagonal:
                w = jnp.where(before, w, 0.0)
            acc_ref[sl, :] += jnp.dot(w.astype(BF16), v, preferred_element_type=F32)
            c_new = c + sums[:, tq:]
            c_ref[sl, :] = c_new
            cmax = jnp.maximum(cmax, jnp.max(c_new))
        return cmax

    first = block(i, True)
    lax.while_loop(lambda cr: (cr[0] >= 0) & (cr[1] > SB_STOP), lambda cr: (cr[0] - 1, block(cr[0], False)),
                   (i - 1, first))
    out = jnp.zeros((tq, BRANCH_WIDTH), F32)
    for hd in range(N_HEADS):
        out = jnp.where(head == hd, acc_ref[hd * tq:(hd + 1) * tq, :], out)
    o_ref[...] = out.astype(BF16)


def _sb_attention(sb, tq=256):
    b, s, _ = sb.shape
    return pl.pallas_call(
        functools.partial(_sb_kernel, tq=tq),
        grid=(b, s // tq),
        in_specs=[pl.BlockSpec((None, tq, BRANCH_WIDTH), lambda bi, i: (bi, i, 0)),
                  pl.BlockSpec((None, s, BRANCH_WIDTH), lambda bi, i: (bi, 0, 1)),
                  pl.BlockSpec((None, s, BRANCH_WIDTH), lambda bi, i: (bi, 0, 2))],
        out_specs=pl.BlockSpec((None, tq, BRANCH_WIDTH), lambda bi, i: (bi, i, 0)),
        out_shape=jax.ShapeDtypeStruct((b, s, BRANCH_WIDTH), BF16),
        scratch_shapes=[pltpu.VMEM((N_HEADS * tq, BRANCH_WIDTH), F32), pltpu.VMEM((N_HEADS * tq, LANES), F32)],
        compiler_params=_cparams("parallel", "parallel"),
        name="sb_attention",
    )(sb, sb, sb)


def _mla_kernel(q_ref, k_ref, v_ref, o_ref, m_ref, acc_ref, *, t):
    i = pl.program_id(1)
    row = lax.broadcasted_iota(jnp.int32, (t, t), 0)
    col = lax.broadcasted_iota(jnp.int32, (t, t), 1)
    m_ref[...] = jnp.full(m_ref.shape, NEG, F32)
    acc_ref[...] = jnp.zeros_like(acc_ref)

    def step(j, masked, span=1):
        rows = pl.ds(pl.multiple_of(j * t, t), span * t)
        for hd in range(N_HEADS):
            sl = slice(hd * LANES, (hd + 1) * LANES)
            s = lax.dot_general(q_ref[:, sl], k_ref[rows, sl], _NT, preferred_element_type=F32)
            if masked:
                s = jnp.where(col <= row, s, NEG)
            m_prev = m_ref[hd]
            m_new = jnp.maximum(m_prev, jnp.max(s, axis=1, keepdims=True))
            p = jnp.exp2((s - jnp.concatenate([m_new] * (span * t // LANES), axis=1)).astype(BF16))
            acc_ref[hd] = jnp.exp2(m_prev - m_new) * acc_ref[hd] + jnp.dot(p, v_ref[rows, sl],
                                                                           preferred_element_type=F32)
            m_ref[hd] = m_new

    def wide(j, carry):
        step(MLA_SPAN * j, False, span=MLA_SPAN)
        return carry

    lax.fori_loop(0, i // MLA_SPAN, wide, 0)
    done = (i // MLA_SPAN) * MLA_SPAN
    span = MLA_SPAN // 2
    while span >= 1:
        pl.when((i - done) % (2 * span) >= span)(functools.partial(step, done, False, span))
        done = done + jnp.where((i - done) % (2 * span) >= span, span, 0)
        span //= 2
    step(i, True)
    for hd in range(N_HEADS):
        acc = acc_ref[hd]
        o_ref[:, hd * LANES:(hd + 1) * LANES] = (acc / acc[:, MLA_V:MLA_V + 1]).astype(BF16)


def _mla_attention(m, t=512):
    b, s, _ = m.shape
    w = N_HEADS * LANES
    t = min(t, s)
    return pl.pallas_call(
        functools.partial(_mla_kernel, t=t),
        grid=(b, s // t),
        in_specs=[pl.BlockSpec((None, t, w), lambda bi, i: (bi, i, 0)),
                  pl.BlockSpec((None, s, w), lambda bi, i: (bi, 0, 1)),
                  pl.BlockSpec((None, s, w), lambda bi, i: (bi, 0, 2))],
        out_specs=pl.BlockSpec((None, t, w), lambda bi, i: (bi, i, 0)),
        out_shape=jax.ShapeDtypeStruct((b, s, w), BF16),
        scratch_shapes=[pltpu.VMEM((N_HEADS, t, LANES), F32), pltpu.VMEM((N_HEADS, t, LANES), F32)],
        compiler_params=_cparams("parallel", "parallel"),
        name="mla_attention",
    )(m, m, m)


def _small_unroll(n):
    return next((u for u in (4, 5, 3, 2) if n % u == 0), 1)


def _dil_kernel(cur_ref, kp_ref, vp_ref, out_ref, og_ref, lg_ref):
    g = pl.program_id(2)
    has_prev_chunk = pl.program_id(1) > 0
    L, C = DIL_L, DIL_CHUNK
    halves = BRANCH_WIDTH // LANES
    head = _head_of_lane(BRANCH_WIDTH, HEAD_DIM)
    a = lax.broadcasted_iota(jnp.int32, (N_HEADS * L, 2 * L), 0) % L
    c = lax.broadcasted_iota(jnp.int32, (N_HEADS * L, 2 * L), 1)
    band = (c >= a) & (c <= a + L)
    band_first = band & ((c >= L) | has_prev_chunk)

    def rows_of(start, d):
        return pl.ds(start, L) if d == 1 else pl.ds(start, L, stride=d)

    def load(ref, slab, start, d):
        idx = rows_of(start, d)
        return jnp.concatenate([ref[slab + p, idx, :] for p in range(halves)], axis=1).astype(BF16)

    def unit(gi, d, q_start, kprev, vprev, valid):
        q = load(cur_ref, 0, q_start, d)
        kw = jnp.concatenate([kprev, load(cur_ref, halves, q_start, d)], axis=0)
        vw = jnp.concatenate([vprev, load(cur_ref, 2 * halves, q_start, d)], axis=0)
        qs = jnp.concatenate([jnp.where(head == hd, q, jnp.zeros_like(q)) for hd in range(N_HEADS)], axis=0)
        sc = lax.dot_general(qs, kw, _NT, preferred_element_type=F32)
        sc = jnp.where(valid, sc, NEG)
        mx = jnp.max(sc, axis=1, keepdims=True)
        p = jnp.exp(sc - mx)
        sm = jnp.sum(p, axis=1, keepdims=True)
        oh = jnp.dot((p * (1.0 / sm)).astype(BF16), vw, preferred_element_type=F32)
        lse = mx + jnp.log(sm)
        o = jnp.zeros((L, BRANCH_WIDTH), F32)
        lw = jnp.zeros((L, BRANCH_WIDTH), F32)
        for hd in range(N_HEADS):
            o = jnp.where(head == hd, oh[hd * L:(hd + 1) * L], o)
            lw = jnp.where(head == hd, lse[hd * L:(hd + 1) * L], lw)
        idx = rows_of(q_start, d)
        for p_ in range(halves):
            og_ref[gi, p_, idx, :] = o[:, p_ * LANES:(p_ + 1) * LANES]
            lg_ref[gi, p_, idx, :] = lw[:, p_ * LANES:(p_ + 1) * LANES]

    def group(gi, d):
        nb = C // (L * d)

        def first(r, carry):
            start = (nb - 1) * L * d + r
            unit(gi, d, r, load(kp_ref, 0, start, d), load(vp_ref, 0, start, d), band_first)
            return carry

        lax.fori_loop(0, d, first, 0, unroll=_small_unroll(d))

        def rest(idx, carry):
            start = (1 + idx // d) * (L * d) + idx % d
            prev = start - L * d
            unit(gi, d, start, load(cur_ref, halves, prev, d), load(cur_ref, 2 * halves, prev, d), band)
            return carry

        if nb > 1:
            lax.fori_loop(0, d * (nb - 1), rest, 0, unroll=_small_unroll(d * (nb - 1)))

    for gi, (_, d) in enumerate(DIL_CONFIGS):
        pl.when(g == gi)(functools.partial(group, gi, d))

    @pl.when(g == N_DIL - 1)
    def _():
        for p_ in range(halves):
            lgs = [lg_ref[gi, p_] for gi in range(N_DIL)]
            mx = functools.reduce(jnp.maximum, lgs)
            es = [jnp.exp(l - mx) for l in lgs]
            num = functools.reduce(lambda u, w: u + w, [es[gi] * og_ref[gi, p_] for gi in range(N_DIL)])
            den = functools.reduce(lambda u, w: u + w, es)
            out_ref[:, p_ * LANES:(p_ + 1) * LANES] = (num / den).astype(BF16)


def _dil_attention(dil):
    b, slabs, s, _ = dil.shape
    C = DIL_CHUNK
    assert slabs == DIL_SLABS and s % C == 0 and all(w // d == DIL_L and C % w == 0 for w, d in DIL_CONFIGS)
    per_group = DIL_SLABS // N_DIL
    halves = BRANCH_WIDTH // LANES

    def prev(t):
        return pl.BlockSpec((None, halves, C, LANES),
                            lambda bi, ch, g: (bi, 3 * g + t, jnp.maximum(ch - 1, 0), 0))

    scratch = pltpu.VMEM((N_DIL, halves, C, LANES), F32)
    return pl.pallas_call(
        _dil_kernel,
        grid=(b, s // C, N_DIL),
        in_specs=[pl.BlockSpec((None, per_group, C, LANES), lambda bi, ch, g: (bi, g, ch, 0)), prev(1), prev(2)],
        out_specs=pl.BlockSpec((None, C, BRANCH_WIDTH), lambda bi, ch, g: (bi, ch, 0)),
        out_shape=jax.ShapeDtypeStruct((b, s, BRANCH_WIDTH), BF16),
        scratch_shapes=[scratch] * 2,
        compiler_params=_cparams("parallel", "parallel", "arbitrary"),
        name="dil_attention",
    )(dil, dil, dil)


def _merge_kernel(x_ref, oa_ref, ob_ref, oc_ref, g_ref, wa_ref, wb_ref, wc_ref, wo_ref, out_ref):
    merged = g_ref[:, :D_MODEL].astype(F32) * jnp.dot(oa_ref[...], wa_ref[...], preferred_element_type=F32)
    merged += g_ref[:, D_MODEL:2 * D_MODEL].astype(F32) * jnp.dot(ob_ref[...], wb_ref[...], preferred_element_type=F32)
    merged += g_ref[:, 2 * D_MODEL:].astype(F32) * jnp.dot(oc_ref[...], wc_ref[...], preferred_element_type=F32)
    out_ref[...] = x_ref[...] + jnp.dot(merged.astype(BF16), wo_ref[...], preferred_element_type=F32)


def _merge(x, oa, ob, oc, gates, layer, wa, wb, wc, wo, tm=512):
    t = x.shape[0]
    row = lambda w: pl.BlockSpec((tm, w), lambda i: (i, 0))
    full = lambda a: _layer_spec(a, layer)
    return pl.pallas_call(
        _merge_kernel,
        grid=(t // tm,),
        in_specs=[row(D_MODEL), row(BRANCH_WIDTH), row(N_HEADS * LANES), row(BRANCH_WIDTH), row(GATE_COLS),
                  full(wa), full(wb), full(wc), full(wo)],
        out_specs=row(D_MODEL),
        out_shape=jax.ShapeDtypeStruct((t, D_MODEL), F32),
        compiler_params=_cparams("parallel"),
        name="merge",
    )(x, oa, ob, oc, gates, wa, wb, wc, wo)


def _route(logits_t):
    col = lambda k: logits_t[k:k + 1, :]
    gl = [col(k) for k in range(N_GROUPS)]
    gmax = functools.reduce(jnp.maximum, gl)
    taken = jnp.zeros_like(gmax) > 1.0
    is_g = []
    for k in range(N_GROUPS):
        hit = jnp.logical_and(gl[k] == gmax, jnp.logical_not(taken))
        is_g.append(hit)
        taken = jnp.logical_or(taken, hit)
    p_group = 1.0 / functools.reduce(lambda u, w: u + w, [jnp.exp(g - gmax) for g in gl])
    a = []
    for e in range(EXPERTS_PER_GROUP):
        v = col(N_GROUPS + (N_GROUPS - 1) * EXPERTS_PER_GROUP + e)
        for k in range(N_GROUPS - 2, -1, -1):
            v = jnp.where(is_g[k], col(N_GROUPS + k * EXPERTS_PER_GROUP + e), v)
        a.append(v)
    t1 = functools.reduce(jnp.maximum, a)
    taken = jnp.zeros_like(t1) > 1.0
    first = []
    for e in range(EXPERTS_PER_GROUP):
        hit = jnp.logical_and(a[e] == t1, jnp.logical_not(taken))
        first.append(hit)
        taken = jnp.logical_or(taken, hit)
    rest = [jnp.where(first[e], -jnp.inf, a[e]) for e in range(EXPERTS_PER_GROUP)]
    t2 = functools.reduce(jnp.maximum, rest)
    taken = jnp.zeros_like(t1) > 1.0
    second = []
    for e in range(EXPERTS_PER_GROUP):
        hit = jnp.logical_and(jnp.logical_and(rest[e] == t2, jnp.logical_not(first[e])), jnp.logical_not(taken))
        second.append(hit)
        taken = jnp.logical_or(taken, hit)
    d = jnp.exp(t2 - t1)
    p1 = 1.0 / (1.0 + d)
    p2 = d / (1.0 + d)
    w = [p_group * jnp.where(first[e], p1, jnp.where(second[e], p2, 0.0)) for e in range(EXPERTS_PER_GROUP)]
    return is_g, [[jnp.where(is_g[k], w[e], 0.0) for e in range(EXPERTS_PER_GROUP)] for k in range(N_GROUPS)]


def _moe_kernel(x_ref, g_ref, wr_ref, br_ref, wg_ref, wu_ref, wd_ref, gfin_ref, out_ref,
                h_ref, comb_ref, dest_row_ref, dest_col_ref, cnt_ref, acc_ref, *, final):
    gi = pl.program_id(1)
    tm = x_ref.shape[0]

    @pl.when(gi == 0)
    def _():
        x = x_ref[...]
        hf = _rms(x, g_ref[...])
        h_hi = hf.astype(BF16)
        h_lo = (hf - h_hi.astype(F32)).astype(BF16)
        hi = lax.dot_general(wr_ref[...], h_hi, _NT, preferred_element_type=F32)
        lo = lax.dot_general(wr_ref[:ROUTER_ROWS, :], h_lo, _NT, preferred_element_type=F32)
        is_g, comb = _route(hi[:ROUTER_ROWS] + hi[ROUTER_ROWS:] + lo + br_ref[...])
        sub = lax.broadcasted_iota(jnp.int32, (LANES, 1), 0)
        for k in range(N_GROUPS):
            comb_t = jnp.zeros((LANES, tm), F32)
            for e in range(EXPERTS_PER_GROUP):
                comb_t = jnp.where(sub == e, comb[k][e], comb_t)
            c = comb_t.T
            c_hi = c.astype(BF16)
            comb_ref[k] = jnp.concatenate([c_hi, (c - c_hi.astype(F32)).astype(BF16)], axis=1)
        sub8 = lax.broadcasted_iota(jnp.int32, (8, 1), 0)
        onehot = jnp.zeros((8, tm), F32)
        for k in range(N_GROUPS):
            onehot = jnp.where((sub8 == k) & is_g[k], 1.0, onehot)
        earlier = (lax.broadcasted_iota(jnp.int32, (tm, tm), 0)
                   < lax.broadcasted_iota(jnp.int32, (tm, tm), 1))
        ranks = jnp.dot(onehot.astype(BF16), jnp.where(earlier, 1.0, 0.0).astype(BF16),
                        preferred_element_type=F32)
        dest = jnp.zeros((1, tm), F32)
        for k in range(N_GROUPS):
            dest = jnp.where(is_g[k], ranks[k:k + 1, :] + float(k * tm), dest)
            cnt_ref[k] = jnp.sum(jnp.where(is_g[k], 1.0, 0.0)).astype(jnp.int32)
        dest_row_ref[...] = dest
        dest_col_ref[...] = jnp.broadcast_to(dest, (LANES, tm)).T
        h_ref[...] = h_hi
        acc_ref[...] = x

    for chunk in range(tm // MOE_SLOTS):
        @pl.when(cnt_ref[gi] > chunk * MOE_SLOTS)
        def _():
            first = (gi * tm + chunk * MOE_SLOTS).astype(F32)
            slot_r = first + lax.broadcasted_iota(jnp.int32, (MOE_SLOTS, 1), 0).astype(F32)
            pick = jnp.where(dest_row_ref[...] == slot_r, 1.0, 0.0).astype(BF16)
            hs = jnp.dot(pick, h_ref[...], preferred_element_type=F32).astype(BF16)
            cs = jnp.dot(pick, comb_ref[gi], preferred_element_type=F32)
            comb = cs[:, :LANES] + cs[:, LANES:]
            parts = []
            for e in range(EXPERTS_PER_GROUP):
                gate = jnp.dot(hs, wg_ref[e], preferred_element_type=F32)
                up = jnp.dot(hs, wu_ref[e], preferred_element_type=F32)
                parts.append((gate * (1.0 / (1.0 + jnp.exp(-gate))) * up * comb[:, e:e + 1]).astype(BF16))
            y = jnp.dot(jnp.concatenate(parts, axis=1), wd_ref[...], preferred_element_type=F32)
            slot_c = first + lax.broadcasted_iota(jnp.int32, (1, MOE_SLOTS), 1).astype(F32)
            dcol = jnp.concatenate([dest_col_ref[...]] * (MOE_SLOTS // LANES), axis=1)
            place = jnp.where(dcol == slot_c, 1.0, 0.0).astype(BF16)
            acc_ref[...] += jnp.dot(place, y.astype(BF16), preferred_element_type=F32)

    @pl.when(gi == N_GROUPS - 1)
    def _():
        y = acc_ref[...]
        out_ref[...] = _rms(y, gfin_ref[...]) if final else y


def _moe(x, layer, g_ffn, wr, br, wg, wu, wd, g_final, final, tm=1024):
    t = x.shape[0]
    row = pl.BlockSpec((tm, D_MODEL), lambda i, k: (i, 0))
    full = lambda a: _layer_spec(a, layer)
    grp_in = pl.BlockSpec((EXPERTS_PER_GROUP, D_MODEL, D_EXPERT), lambda i, k: (layer * N_GROUPS + k, 0, 0))
    grp_out = pl.BlockSpec((None, EXPERTS_PER_GROUP * D_EXPERT, D_MODEL), lambda i, k: (layer * N_GROUPS + k, 0, 0))
    return pl.pallas_call(
        functools.partial(_moe_kernel, final=final),
        grid=(t // tm, N_GROUPS),
        in_specs=[row, full(g_ffn), full(wr), full(br), grp_in, grp_in, grp_out,
                  pl.BlockSpec(g_final.shape, lambda i, k: (0, 0))],
        out_specs=row,
        out_shape=jax.ShapeDtypeStruct((t, D_MODEL), F32),
        scratch_shapes=[pltpu.VMEM((tm, D_MODEL), BF16), pltpu.VMEM((N_GROUPS, tm, 2 * LANES), BF16),
                        pltpu.VMEM((1, tm), F32), pltpu.VMEM((tm, LANES), F32), pltpu.SMEM((N_GROUPS,), jnp.int32),
                        pltpu.VMEM((tm, D_MODEL), F32)],
        compiler_params=_cparams("parallel", "arbitrary"),
        name="moe",
    )(x, g_ffn, wr, br, wg, wu, wd, g_final)


def _pad_heads(w, per_head, keep):
    lead = w.shape[:-1]
    w = w.reshape(*lead, N_HEADS, per_head)[..., keep]
    w = jnp.pad(w, [(0, 0)] * (w.ndim - 1) + [(0, LANES - w.shape[-1])])
    return w.reshape(*lead, N_HEADS * LANES)


def _prepare_weights(w_in, w_uq, w_ukv, w_branch, w_out, w_group_router, b_group_router, w_expert_router,
                     b_expert_router, w_gate, w_up, w_down):
    depth = w_in.shape[0]
    w = jnp.pad(w_in.astype(BF16), ((0, 0), (0, 0), (0, -w_in.shape[2] % LANES)))
    wuq = _pad_heads(w_uq, MLA_NOPE + MLA_ROPE, slice(None)).astype(BF16)
    wuk = _pad_heads(w_ukv, 2 * MLA_NOPE, slice(0, MLA_NOPE)).astype(BF16)
    wuv = _pad_heads(w_ukv, 2 * MLA_NOPE, slice(MLA_NOPE, None)).astype(BF16)
    wa = w_branch[:, 0].astype(BF16)
    wb = w_branch[:, 1].reshape(depth, N_HEADS, HEAD_DIM, D_MODEL)
    wb = jnp.pad(wb, ((0, 0), (0, 0), (0, LANES - HEAD_DIM), (0, 0))).reshape(depth, N_HEADS * LANES, D_MODEL)
    wb = wb.astype(BF16)
    wc = w_branch[:, 2].astype(BF16)
    wo = w_out.astype(BF16)
    n_exp = N_GROUPS * EXPERTS_PER_GROUP
    wr = jnp.concatenate([w_group_router, w_expert_router], axis=2).transpose(0, 2, 1)
    wr = jnp.pad(wr, ((0, 0), (0, ROUTER_ROWS - N_GROUPS - n_exp), (0, 0)))
    wr_hi = wr.astype(BF16)
    wr = jnp.concatenate([wr_hi, (wr - wr_hi.astype(F32)).astype(BF16)], axis=1)
    br = jnp.concatenate([b_group_router, b_expert_router], axis=1)
    br = jnp.pad(br, ((0, 0), (0, ROUTER_ROWS - N_GROUPS - n_exp)))[:, :, None]

    wgate = w_gate.astype(BF16).reshape(depth * n_exp, D_MODEL, D_EXPERT)
    wup = w_up.astype(BF16).reshape(depth * n_exp, D_MODEL, D_EXPERT)
    wdown = w_down.astype(BF16).reshape(depth * N_GROUPS, EXPERTS_PER_GROUP * D_EXPERT, D_MODEL)
    return dict(w=w, wuq=wuq, wuk=wuk, wuv=wuv, wa=wa, wb=wb, wc=wc, wo=wo,
                wr=wr, br=br, wgate=wgate, wup=wup, wdown=wdown)


def kernel(x, positions, g_mix, w_in, g_q_lat, w_uq, g_kv_lat, w_ukv, w_branch, w_out, g_ffn, w_group_router, b_group_router, w_expert_router, b_expert_router, w_gate, w_up, w_down, g_final):
    b, s, d = x.shape
    depth = w_in.shape[0]
    assert d == D_MODEL and w_in.shape[2] == SB_COLS + MLA_COLS + DIL_COLS + GATE_COLS
    t = b * s
    w = _prepare_weights(w_in, w_uq, w_ukv, w_branch, w_out, w_group_router, b_group_router, w_expert_router,
                         b_expert_router, w_gate, w_up, w_down)
    tabs = _rope_tables(positions.reshape(t, 1).astype(F32))
    xf = x.reshape(t, d)
    g_fin = g_final.reshape(1, d)
    gains = [g[:, None, :] for g in (g_mix, g_q_lat, g_kv_lat, g_ffn)]
    for l in range(depth):
        sb, m, dil, gates = _proj(xf, b, l, gains[0], w["w"], gains[1], w["wuq"], gains[2], w["wuk"], w["wuv"], tabs)
        oa = _sb_attention(sb.reshape(b, s, -1)).reshape(t, -1)
        ob = _mla_attention(m.reshape(b, s, -1)).reshape(t, -1)
        oc = _dil_attention(dil).reshape(t, -1)
        xf = _merge(xf, oa, ob, oc, gates, l, w["wa"], w["wb"], w["wc"], w["wo"])
        xf = _moe(xf, l, gains[3], w["wr"], w["br"], w["wgate"], w["wup"], w["wdown"], g_fin,
                  final=(l == depth - 1))
    return xf.reshape(b, s, d)
```

```python
import functools
import math

import jax
import jax.numpy as jnp
from jax import lax
from jax.experimental import pallas as pl
from jax.experimental.pallas import tpu as pltpu

F32 = jnp.float32
BF16 = jnp.bfloat16

D_MODEL = 1024
HEAD_DIM = 64
N_HEADS = 4
BRANCH_WIDTH = N_HEADS * HEAD_DIM
ROPE_THETA = 10000.0
EPS = 1e-6
NEG = -1e30
MLA_NOPE = 64
MLA_ROPE = 32
MLA_Q_LORA = 256
MLA_KV_LORA = 128
DIL_CONFIGS = ((128, 1), (512, 4), (2048, 16))
DIL_L = 128
N_DIL = len(DIL_CONFIGS)
N_BRANCH = 3
N_GROUPS = 4
EXPERTS_PER_GROUP = 4
D_EXPERT = 256
ROUTER_ROWS = 32
SB_COLS = 3 * BRANCH_WIDTH
MLA_COLS = MLA_Q_LORA + MLA_KV_LORA + MLA_ROPE
DIL_COLS = N_DIL * 3 * BRANCH_WIDTH
GATE_COLS = N_BRANCH * D_MODEL
MLA_V = 64
LOG2_E = math.log2(math.e)

LANES = 128
VMEM_LIMIT = 56 * 1024 * 1024
DIL_SLABS = DIL_COLS // LANES
W_SHIFT = (SB_COLS + MLA_COLS) % LANES
assert SB_COLS % LANES == 0 and DIL_COLS % LANES == 0 and W_SHIFT == MLA_ROPE
DIL_CHUNK = DIL_CONFIGS[-1][0]

SB_STOP = -110.0
SB_STACKS = 2
MLA_SPAN = 4
MOE_SPILL = 64
MOE_SLOTS = 256

_NT = (((1,), (1,)), ((), ()))


def _cparams(*sem):
    return pltpu.CompilerParams(dimension_semantics=sem, vmem_limit_bytes=VMEM_LIMIT)


def _head_of_lane(width, per_head):
    return lax.broadcasted_iota(jnp.int32, (1, width), 1) // per_head


def _rope_table_kernel(pos_ref, dc_ref, dsa_ref, dsb_ref, mc_ref, msa_ref, msb_ref):
    pos = pos_ref[...]
    lane = lax.broadcasted_iota(jnp.int32, (1, LANES), 1)
    dh, mh = HEAD_DIM // 2, MLA_ROPE // 2
    is_d, is_m = lane < dh, (lane >= dh) & (lane < dh + mh)
    expo = jnp.where(is_d, lane.astype(F32) * (-2.0 / HEAD_DIM), (lane - dh).astype(F32) * (-2.0 / MLA_ROPE))
    ang = pos * jnp.exp(expo * math.log(ROPE_THETA))
    cos, sin = jnp.cos(ang), jnp.sin(ang)

    def spread(x, keep, shifts):
        x = jnp.where(keep, x, 0.0)
        return functools.reduce(lambda u, w: u + w, [pltpu.roll(x, s, 1) if s else x for s in shifts])

    d_shifts = tuple(range(0, LANES, dh))
    dcos, dsin = spread(cos, is_d, d_shifts), spread(sin, is_d, d_shifts)
    first = (lane % HEAD_DIM) < dh
    dc_ref[...], dsa_ref[...], dsb_ref[...] = dcos, jnp.where(first, -dsin, 0.0), jnp.where(first, 0.0, dsin)
    rel = lane - MLA_NOPE
    in_rope = (rel >= 0) & (rel < MLA_ROPE)
    m_shifts = (MLA_NOPE - dh, MLA_NOPE - dh + mh)
    mcos, msin = spread(cos, is_m, m_shifts), spread(sin, is_m, m_shifts)
    mc_ref[...] = jnp.where(in_rope, mcos, 1.0)
    msa_ref[...] = jnp.where(in_rope & (rel < mh), -msin, 0.0)
    msb_ref[...] = jnp.where(in_rope & (rel >= mh), msin, 0.0)


def _rope_tables(pos_col, tm=1024):
    t = pos_col.shape[0]
    spec = pl.BlockSpec((tm, LANES), lambda i: (i, 0))
    return pl.pallas_call(
        _rope_table_kernel,
        grid=(t // tm,),
        in_specs=[pl.BlockSpec((tm, 1), lambda i: (i, 0))],
        out_specs=[spec] * 6,
        out_shape=[jax.ShapeDtypeStruct((t, LANES), F32)] * 6,
        compiler_params=_cparams("parallel"),
        name="rope_tables",
    )(pos_col)


def _rope_block(x, c, sa, sb, half):
    return x * c + pltpu.roll(x, LANES - half, 1) * sa + pltpu.roll(x, half, 1) * sb


def _rms(x, g):
    return x * lax.rsqrt(jnp.mean(x * x, axis=-1, keepdims=True) + EPS) * g


def _shift_roll(block):
    return pltpu.roll(block, LANES - W_SHIFT, 1)


def _shifted(blocks, first_block, n_out, carry, per, last):
    lane = lax.broadcasted_iota(jnp.int32, (1, LANES), 1)
    for j in range(0, n_out, per):
        r = blocks(first_block + j, first_block + j + per)
        for p in range(per):
            nxt = _shift_roll(r[:, p * LANES:(p + 1) * LANES])
            yield j + p, jnp.where(lane < LANES - W_SHIFT, carry, nxt)
            carry = nxt
    last.append(carry)


def _proj_kernel(x_ref, gmix_ref, w_ref, gq_ref, wuq_ref, gkv_ref, wuk_ref,
                 wuv_ref, dc_ref, dsa_ref, dsb_ref, mc_ref, msa_ref, msb_ref,
                 sb_ref, m_ref, dil_ref, g_ref):
    h = _rms(x_ref[...], gmix_ref[...]).astype(BF16)
    q_scale = HEAD_DIM ** -0.5

    def mm(a, w):
        return jnp.dot(a, w, preferred_element_type=F32)

    def blocks(first, last):
        return mm(h, w_ref[:, first * LANES:last * LANES])

    per = BRANCH_WIDTH // LANES
    for part in range(3):
        r = blocks(part * per, (part + 1) * per)
        sb_ref[:, part * BRANCH_WIDTH:(part + 1) * BRANCH_WIDTH] = (r * q_scale if part == 0 else r).astype(BF16)

    lat0 = SB_COLS // LANES
    lat = blocks(lat0, lat0 + 4)
    cq = lat[:, :MLA_Q_LORA]
    ckv = lat[:, MLA_Q_LORA:MLA_Q_LORA + MLA_KV_LORA]
    edge = lat[:, MLA_Q_LORA + MLA_KV_LORA:]
    mc, msa, msb = mc_ref[...], msa_ref[...], msb_ref[...]
    lane = lax.broadcasted_iota(jnp.int32, (1, LANES), 1)
    in_rope = (lane >= MLA_NOPE) & (lane < MLA_NOPE + MLA_ROPE)
    kr = jnp.where(in_rope, _rope_block(pltpu.roll(edge, MLA_NOPE, 1), mc, msa, msb, MLA_ROPE // 2), 0.0)
    q = mm(_rms(cq, gq_ref[...]).astype(BF16), wuq_ref[...])
    ckv_n = _rms(ckv, gkv_ref[...]).astype(BF16)
    kn = mm(ckv_n, wuk_ref[...])
    vv = mm(ckv_n, wuv_ref[...])
    mla_scale = (MLA_NOPE + MLA_ROPE) ** -0.5 * LOG2_E
    for hd in range(N_HEADS):
        sl = slice(hd * LANES, (hd + 1) * LANES)
        qh = _rope_block(q[:, sl], mc, msa, msb, MLA_ROPE // 2) * mla_scale
        m_ref[:, sl] = qh.astype(BF16)
        m_ref[:, N_HEADS * LANES + hd * LANES:N_HEADS * LANES + (hd + 1) * LANES] = (kn[:, sl] + kr).astype(BF16)
    lane4 = lax.broadcasted_iota(jnp.int32, (1, N_HEADS * LANES), 1)
    m_ref[:, 2 * N_HEADS * LANES:] = jnp.where(lane4 % LANES == MLA_V, 1.0, vv).astype(BF16)

    dc, dsa, dsb = dc_ref[...], dsa_ref[...], dsb_ref[...]
    last = []
    for j, blk in _shifted(blocks, lat0 + 4, DIL_SLABS, _shift_roll(edge), per, last):
        part = (j // per) % 3
        if part != 2:
            blk = _rope_block(blk * q_scale if part == 0 else blk, dc, dsa, dsb, HEAD_DIM // 2)
        dil_ref[j] = blk

    for j, blk in _shifted(blocks, lat0 + 4 + DIL_SLABS, GATE_COLS // LANES, last[0], per, last):
        g_ref[:, j * LANES:(j + 1) * LANES] = (1.0 / (1.0 + jnp.exp(-blk))).astype(BF16)


def _layer_spec(a, layer, **kw):
    return pl.BlockSpec((None,) + a.shape[1:], lambda *_: (layer,) + (0,) * (a.ndim - 1), **kw)


def _proj(x, batch, layer, gmix, w, gq, wuq, gkv, wuk, wuv, tabs, tm=512):
    t = x.shape[0]
    per_batch = t // batch // tm
    row = lambda w: pl.BlockSpec((tm, w), lambda i: (i, 0))
    full = lambda a: _layer_spec(a, layer, pipeline_mode=pl.Buffered(1))
    weights = (gmix, w, gq, wuq, gkv, wuk, wuv)
    m_cols = 3 * N_HEADS * LANES
    dil_spec = pl.BlockSpec((None, DIL_SLABS, tm, LANES), lambda i: (i // per_batch, 0, i % per_batch, 0))
    return pl.pallas_call(
        _proj_kernel,
        grid=(t // tm,),
        in_specs=[row(D_MODEL)] + [full(a) for a in weights] + [row(LANES)] * 6,
        out_specs=[row(SB_COLS), row(m_cols), dil_spec, row(GATE_COLS)],
        out_shape=[jax.ShapeDtypeStruct((t, SB_COLS), BF16), jax.ShapeDtypeStruct((t, m_cols), BF16),
                   jax.ShapeDtypeStruct((batch, DIL_SLABS, t // batch, LANES), F32),
                   jax.ShapeDtypeStruct((t, GATE_COLS), BF16)],
        compiler_params=_cparams("parallel"),
        name="proj",
    )(x, *weights, *tabs)


def _sb_kernel(q_ref, k_ref, v_ref, o_ref, acc_ref, c_ref, *, tq):
    i = pl.program_id(1)
    q = q_ref[...]
    head = _head_of_lane(BRANCH_WIDTH, HEAD_DIM)
    per_stack = N_HEADS // SB_STACKS
    qs = [jnp.concatenate([jnp.where(head == st * per_stack + hd, q, jnp.zeros_like(q)) for hd in range(per_stack)],
                          axis=0) for st in range(SB_STACKS)]
    rows = per_stack * tq
    krow = lax.broadcasted_iota(jnp.int32, (tq, tq + LANES), 0)
    kcol = lax.broadcasted_iota(jnp.int32, (tq, tq + LANES), 1)
    later_ones = jnp.where((krow > kcol) | (kcol >= tq), 1.0, 0.0).astype(BF16)
    before = (lax.broadcasted_iota(jnp.int32, (rows, tq), 1)
              < lax.broadcasted_iota(jnp.int32, (rows, tq), 0) % tq)
    acc_ref[...] = jnp.zeros_like(acc_ref)
    c_ref[...] = jnp.zeros_like(c_ref)

    def block(j, diagonal):
        k = k_ref[pl.ds(pl.multiple_of(j * tq, tq), tq), :]
        v = v_ref[pl.ds(pl.multiple_of(j * tq, tq), tq), :]
        cmax = jnp.float32(-jnp.inf)
        for st in range(SB_STACKS):
            sl = slice(st * rows, (st + 1) * rows)
            z = lax.dot_general(qs[st], k, _NT, preferred_element_type=F32)
            sp = jnp.log(1.0 + jnp.exp(-jnp.abs(z)))
            log_beta = jnp.minimum(z, 0.0) - sp
            log_1m = log_beta - z
            if diagonal:
                log_1m = jnp.where(before, log_1m, 0.0)
            sums = jnp.dot(log_1m.astype(BF16), later_ones, preferred_element_type=F32)
            c = c_ref[sl, :]
            w = jnp.exp(log_beta + jnp.concatenate([c] * (tq // LANES), axis=1) + sums[:, :tq])
            if diagonal:
                w = jnp.where(before, w, 0.0)
            acc_ref[sl, :] += jnp.dot(w.astype(BF16), v, preferred_element_type=F32)
            c_new = c + sums[:, tq:]
            c_ref[sl, :] = c_new
            cmax = jnp.maximum(cmax, jnp.max(c_new))
        return cmax

    first = block(i, True)
    lax.while_loop(lambda cr: (cr[0] >= 0) & (cr[1] > SB_STOP), lambda cr: (cr[0] - 1, block(cr[0], False)),
                   (i - 1, first))
    out = jnp.zeros((tq, BRANCH_WIDTH), F32)
    for hd in range(N_HEADS):
        out = jnp.where(head == hd, acc_ref[hd * tq:(hd + 1) * tq, :], out)
    o_ref[...] = out.astype(BF16)


def _sb_attention(sb, tq=256):
    b, s, _ = sb.shape
    return pl.pallas_call(
        functools.partial(_sb_kernel, tq=tq),
        grid=(b, s // tq),
        in_specs=[pl.BlockSpec((None, tq, BRANCH_WIDTH), lambda bi, i: (bi, i, 0)),
                  pl.BlockSpec((None, s, BRANCH_WIDTH), lambda bi, i: (bi, 0, 1)),
                  pl.BlockSpec((None, s, BRANCH_WIDTH), lambda bi, i: (bi, 0, 2))],
        out_specs=pl.BlockSpec((None, tq, BRANCH_WIDTH), lambda bi, i: (bi, i, 0)),
        out_shape=jax.ShapeDtypeStruct((b, s, BRANCH_WIDTH), BF16),
        scratch_shapes=[pltpu.VMEM((N_HEADS * tq, BRANCH_WIDTH), F32), pltpu.VMEM((N_HEADS * tq, LANES), F32)],
        compiler_params=_cparams("parallel", "parallel"),
        name="sb_attention",
    )(sb, sb, sb)


def _mla_kernel(q_ref, k_ref, v_ref, o_ref, m_ref, acc_ref, *, t):
    i = pl.program_id(1)
    row = lax.broadcasted_iota(jnp.int32, (t, t), 0)
    col = lax.broadcasted_iota(jnp.int32, (t, t), 1)
    m_ref[...] = jnp.full(m_ref.shape, NEG, F32)
    acc_ref[...] = jnp.zeros_like(acc_ref)

    def step(j, masked, span=1):
        rows = pl.ds(pl.multiple_of(j * t, t), span * t)
        for hd in range(N_HEADS):
            sl = slice(hd * LANES, (hd + 1) * LANES)
            s = lax.dot_general(q_ref[:, sl], k_ref[rows, sl], _NT, preferred_element_type=F32)
            if masked:
                s = jnp.where(col <= row, s, NEG)
            m_prev = m_ref[hd]
            m_new = jnp.maximum(m_prev, jnp.max(s, axis=1, keepdims=True))
            p = jnp.exp2((s - jnp.concatenate([m_new] * (span * t // LANES), axis=1)).astype(BF16))
            acc_ref[hd] = jnp.exp2(m_prev - m_new) * acc_ref[hd] + jnp.dot(p, v_ref[rows, sl],
                                                                           preferred_element_type=F32)
            m_ref[hd] = m_new

    def wide(j, carry):
        step(MLA_SPAN * j, False, span=MLA_SPAN)
        return carry

    lax.fori_loop(0, i // MLA_SPAN, wide, 0)
    done = (i // MLA_SPAN) * MLA_SPAN
    span = MLA_SPAN // 2
    while span >= 1:
        pl.when((i - done) % (2 * span) >= span)(functools.partial(step, done, False, span))
        done = done + jnp.where((i - done) % (2 * span) >= span, span, 0)
        span //= 2
    step(i, True)
    for hd in range(N_HEADS):
        acc = acc_ref[hd]
        o_ref[:, hd * LANES:(hd + 1) * LANES] = (acc / acc[:, MLA_V:MLA_V + 1]).astype(BF16)


def _mla_attention(m, t=512):
    b, s, _ = m.shape
    w = N_HEADS * LANES
    t = min(t, s)
    return pl.pallas_call(
        functools.partial(_mla_kernel, t=t),
        grid=(b, s // t),
        in_specs=[pl.BlockSpec((None, t, w), lambda bi, i: (bi, i, 0)),
                  pl.BlockSpec((None, s, w), lambda bi, i: (bi, 0, 1)),
                  pl.BlockSpec((None, s, w), lambda bi, i: (bi, 0, 2))],
        out_specs=pl.BlockSpec((None, t, w), lambda bi, i: (bi, i, 0)),
        out_shape=jax.ShapeDtypeStruct((b, s, w), BF16),
        scratch_shapes=[pltpu.VMEM((N_HEADS, t, LANES), F32), pltpu.VMEM((N_HEADS, t, LANES), F32)],
        compiler_params=_cparams("parallel", "parallel"),
        name="mla_attention",
    )(m, m, m)


def _small_unroll(n):
    return next((u for u in (4, 5, 3, 2) if n % u == 0), 1)


def _dil_kernel(cur_ref, kp_ref, vp_ref, out_ref, og_ref, lg_ref):
    g = pl.program_id(2)
    has_prev_chunk = pl.program_id(1) > 0
    L, C = DIL_L, DIL_CHUNK
    halves = BRANCH_WIDTH // LANES
    head = _head_of_lane(BRANCH_WIDTH, HEAD_DIM)
    a = lax.broadcasted_iota(jnp.int32, (N_HEADS * L, 2 * L), 0) % L
    c = lax.broadcasted_iota(jnp.int32, (N_HEADS * L, 2 * L), 1)
    band = (c >= a) & (c <= a + L)
    band_first = band & ((c >= L) | has_prev_chunk)

    def rows_of(start, d):
        return pl.ds(start, L) if d == 1 else pl.ds(start, L, stride=d)

    def load(ref, slab, start, d):
        idx = rows_of(start, d)
        return jnp.concatenate([ref[slab + p, idx, :] for p in range(halves)], axis=1).astype(BF16)

    def unit(gi, d, q_start, kprev, vprev, valid):
        q = load(cur_ref, 0, q_start, d)
        kw = jnp.concatenate([kprev, load(cur_ref, halves, q_start, d)], axis=0)
        vw = jnp.concatenate([vprev, load(cur_ref, 2 * halves, q_start, d)], axis=0)
        qs = jnp.concatenate([jnp.where(head == hd, q, jnp.zeros_like(q)) for hd in range(N_HEADS)], axis=0)
        sc = lax.dot_general(qs, kw, _NT, preferred_element_type=F32)
        sc = jnp.where(valid, sc, NEG)
        mx = jnp.max(sc, axis=1, keepdims=True)
        p = jnp.exp(sc - mx)
        sm = jnp.sum(p, axis=1, keepdims=True)
        oh = jnp.dot((p * (1.0 / sm)).astype(BF16), vw, preferred_element_type=F32)
        lse = mx + jnp.log(sm)
        o = jnp.zeros((L, BRANCH_WIDTH), F32)
        lw = jnp.zeros((L, BRANCH_WIDTH), F32)
        for hd in range(N_HEADS):
            o = jnp.where(head == hd, oh[hd * L:(hd + 1) * L], o)
            lw = jnp.where(head == hd, lse[hd * L:(hd + 1) * L], lw)
        idx = rows_of(q_start, d)
        for p_ in range(halves):
            og_ref[gi, p_, idx, :] = o[:, p_ * LANES:(p_ + 1) * LANES]
            lg_ref[gi, p_, idx, :] = lw[:, p_ * LANES:(p_ + 1) * LANES]

    def group(gi, d):
        nb = C // (L * d)

        def first(r, carry):
            start = (nb - 1) * L * d + r
            unit(gi, d, r, load(kp_ref, 0, start, d), load(vp_ref, 0, start, d), band_first)
            return carry

        lax.fori_loop(0, d, first, 0, unroll=_small_unroll(d))

        def rest(idx, carry):
            start = (1 + idx // d) * (L * d) + idx % d
            prev = start - L * d
            unit(gi, d, start, load(cur_ref, halves, prev, d), load(cur_ref, 2 * halves, prev, d), band)
            return carry

        if nb > 1:
            lax.fori_loop(0, d * (nb - 1), rest, 0, unroll=_small_unroll(d * (nb - 1)))

    for gi, (_, d) in enumerate(DIL_CONFIGS):
        pl.when(g == gi)(functools.partial(group, gi, d))

    @pl.when(g == N_DIL - 1)
    def _():
        for p_ in range(halves):
            lgs = [lg_ref[gi, p_] for gi in range(N_DIL)]
            mx = functools.reduce(jnp.maximum, lgs)
            es = [jnp.exp(l - mx) for l in lgs]
            num = functools.reduce(lambda u, w: u + w, [es[gi] * og_ref[gi, p_] for gi in range(N_DIL)])
            den = functools.reduce(lambda u, w: u + w, es)
            out_ref[:, p_ * LANES:(p_ + 1) * LANES] = (num / den).astype(BF16)


def _dil_attention(dil):
    b, slabs, s, _ = dil.shape
    C = DIL_CHUNK
    assert slabs == DIL_SLABS and s % C == 0 and all(w // d == DIL_L and C % w == 0 for w, d in DIL_CONFIGS)
    per_group = DIL_SLABS // N_DIL
    halves = BRANCH_WIDTH // LANES

    def prev(t):
        return pl.BlockSpec((None, halves, C, LANES),
                            lambda bi, ch, g: (bi, 3 * g + t, jnp.maximum(ch - 1, 0), 0))

    scratch = pltpu.VMEM((N_DIL, halves, C, LANES), F32)
    return pl.pallas_call(
        _dil_kernel,
        grid=(b, s // C, N_DIL),
        in_specs=[pl.BlockSpec((None, per_group, C, LANES), lambda bi, ch, g: (bi, g, ch, 0)), prev(1), prev(2)],
        out_specs=pl.BlockSpec((None, C, BRANCH_WIDTH), lambda bi, ch, g: (bi, ch, 0)),
        out_shape=jax.ShapeDtypeStruct((b, s, BRANCH_WIDTH), BF16),
        scratch_shapes=[scratch] * 2,
        compiler_params=_cparams("parallel", "parallel", "arbitrary"),
        name="dil_attention",
    )(dil, dil, dil)


def _merge_kernel(x_ref, oa_ref, ob_ref, oc_ref, g_ref, wa_ref, wb_ref, wc_ref, wo_ref, out_ref):
    merged = g_ref[:, :D_MODEL].astype(F32) * jnp.dot(oa_ref[...], wa_ref[...], preferred_element_type=F32)
    merged += g_ref[:, D_MODEL:2 * D_MODEL].astype(F32) * jnp.dot(ob_ref[...], wb_ref[...], preferred_element_type=F32)
    merged += g_ref[:, 2 * D_MODEL:].astype(F32) * jnp.dot(oc_ref[...], wc_ref[...], preferred_element_type=F32)
    out_ref[...] = x_ref[...] + jnp.dot(merged.astype(BF16), wo_ref[...], preferred_element_type=F32)


def _merge(x, oa, ob, oc, gates, layer, wa, wb, wc, wo, tm=512):
    t = x.shape[0]
    row = lambda w: pl.BlockSpec((tm, w), lambda i: (i, 0))
    full = lambda a: _layer_spec(a, layer)
    return pl.pallas_call(
        _merge_kernel,
        grid=(t // tm,),
        in_specs=[row(D_MODEL), row(BRANCH_WIDTH), row(N_HEADS * LANES), row(BRANCH_WIDTH), row(GATE_COLS),
                  full(wa), full(wb), full(wc), full(wo)],
        out_specs=row(D_MODEL),
        out_shape=jax.ShapeDtypeStruct((t, D_MODEL), F32),
        compiler_params=_cparams("parallel"),
        name="merge",
    )(x, oa, ob, oc, gates, wa, wb, wc, wo)


def _route(logits_t):
    col = lambda k: logits_t[k:k + 1, :]
    gl = [col(k) for k in range(N_GROUPS)]
    gmax = functools.reduce(jnp.maximum, gl)
    taken = jnp.zeros_like(gmax) > 1.0
    is_g = []
    for k in range(N_GROUPS):
        hit = jnp.logical_and(gl[k] == gmax, jnp.logical_not(taken))
        is_g.append(hit)
        taken = jnp.logical_or(taken, hit)
    p_group = 1.0 / functools.reduce(lambda u, w: u + w, [jnp.exp(g - gmax) for g in gl])
    a = []
    for e in range(EXPERTS_PER_GROUP):
        v = col(N_GROUPS + (N_GROUPS - 1) * EXPERTS_PER_GROUP + e)
        for k in range(N_GROUPS - 2, -1, -1):
            v = jnp.where(is_g[k], col(N_GROUPS + k * EXPERTS_PER_GROUP + e), v)
        a.append(v)
    t1 = functools.reduce(jnp.maximum, a)
    taken = jnp.zeros_like(t1) > 1.0
    first = []
    for e in range(EXPERTS_PER_GROUP):
        hit = jnp.logical_and(a[e] == t1, jnp.logical_not(taken))
        first.append(hit)
        taken = jnp.logical_or(taken, hit)
    rest = [jnp.where(first[e], -jnp.inf, a[e]) for e in range(EXPERTS_PER_GROUP)]
    t2 = functools.reduce(jnp.maximum, rest)
    taken = jnp.zeros_like(t1) > 1.0
    second = []
    for e in range(EXPERTS_PER_GROUP):
        hit = jnp.logical_and(jnp.logical_and(rest[e] == t2, jnp.logical_not(first[e])), jnp.logical_not(taken))
        second.append(hit)
        taken = jnp.logical_or(taken, hit)
    d = jnp.exp(t2 - t1)
    p1 = 1.0 / (1.0 + d)
    p2 = d / (1.0 + d)
    w = [p_group * jnp.where(first[e], p1, jnp.where(second[e], p2, 0.0)) for e in range(EXPERTS_PER_GROUP)]
    return is_g, [[jnp.where(is_g[k], w[e], 0.0) for e in range(EXPERTS_PER_GROUP)] for k in range(N_GROUPS)]


def _moe_chunks(tm):
    sizes, start = [], 0
    while start < tm:
        size = min(MOE_SPILL if len(sizes) == 1 else MOE_SLOTS, tm - start)
        sizes.append((start, size))
        start += size
    return sizes


def _moe_kernel(x_ref, g_ref, wr_ref, br_ref, wg_ref, wu_ref, wd_ref, gfin_ref, out_ref,
                h_ref, comb_ref, dest_row_ref, dest_col_ref, cnt_ref, acc_ref, *, final):
    gi = pl.program_id(1)
    tm = x_ref.shape[0]

    @pl.when(gi == 0)
    def _():
        x = x_ref[...]
        hf = _rms(x, g_ref[...])
        h_hi = hf.astype(BF16)
        h_lo = (hf - h_hi.astype(F32)).astype(BF16)
        hi = lax.dot_general(wr_ref[...], h_hi, _NT, preferred_element_type=F32)
        lo = lax.dot_general(wr_ref[:ROUTER_ROWS, :], h_lo, _NT, preferred_element_type=F32)
        is_g, comb = _route(hi[:ROUTER_ROWS] + hi[ROUTER_ROWS:] + lo + br_ref[...])
        sub = lax.broadcasted_iota(jnp.int32, (LANES, 1), 0)
        for k in range(N_GROUPS):
            comb_t = jnp.zeros((LANES, tm), F32)
            for e in range(EXPERTS_PER_GROUP):
                comb_t = jnp.where(sub == e, comb[k][e], comb_t)
            c = comb_t.T
            c_hi = c.astype(BF16)
            comb_ref[k] = jnp.concatenate([c_hi, (c - c_hi.astype(F32)).astype(BF16)], axis=1)
        sub8 = lax.broadcasted_iota(jnp.int32, (8, 1), 0)
        onehot = jnp.zeros((8, tm), F32)
        for k in range(N_GROUPS):
            onehot = jnp.where((sub8 == k) & is_g[k], 1.0, onehot)
        earlier = (lax.broadcasted_iota(jnp.int32, (tm, tm), 0)
                   < lax.broadcasted_iota(jnp.int32, (tm, tm), 1))
        ranks = jnp.dot(onehot.astype(BF16), jnp.where(earlier, 1.0, 0.0).astype(BF16),
                        preferred_element_type=F32)
        dest = jnp.zeros((1, tm), F32)
        for k in range(N_GROUPS):
            dest = jnp.where(is_g[k], ranks[k:k + 1, :] + float(k * tm), dest)
            cnt_ref[k] = jnp.sum(jnp.where(is_g[k], 1.0, 0.0)).astype(jnp.int32)
        dest_row_ref[...] = dest
        dest_col_ref[...] = jnp.broadcast_to(dest, (LANES, tm)).T
        h_ref[...] = h_hi
        acc_ref[...] = x

    for start, slots in _moe_chunks(tm):
        @pl.when(cnt_ref[gi] > start)
        def _(start=start, slots=slots):
            first = (gi * tm + start).astype(F32)
            slot_r = first + lax.broadcasted_iota(jnp.int32, (slots, 1), 0).astype(F32)
            pick = jnp.where(dest_row_ref[...] == slot_r, 1.0, 0.0).astype(BF16)
            hs = jnp.dot(pick, h_ref[...], preferred_element_type=F32).astype(BF16)
            cs = jnp.dot(pick, comb_ref[gi], preferred_element_type=F32)
            comb = cs[:, :LANES] + cs[:, LANES:]
            parts = []
            for e in range(EXPERTS_PER_GROUP):
                gate = jnp.dot(hs, wg_ref[e], preferred_element_type=F32)
                up = jnp.dot(hs, wu_ref[e], preferred_element_type=F32)
                parts.append((gate * (1.0 / (1.0 + jnp.exp(-gate))) * up * comb[:, e:e + 1]).astype(BF16))
            y = jnp.dot(jnp.concatenate(parts, axis=1), wd_ref[...], preferred_element_type=F32)
            slot_c = first + lax.broadcasted_iota(jnp.int32, (1, slots), 1).astype(F32)
            dcol = jnp.concatenate([dest_col_ref[...]] * -(-slots // LANES), axis=1)[:, :slots]
            place = jnp.where(dcol == slot_c, 1.0, 0.0).astype(BF16)
            acc_ref[...] += jnp.dot(place, y.astype(BF16), preferred_element_type=F32)

    @pl.when(gi == N_GROUPS - 1)
    def _():
        y = acc_ref[...]
        out_ref[...] = _rms(y, gfin_ref[...]) if final else y


def _moe(x, layer, g_ffn, wr, br, wg, wu, wd, g_final, final, tm=1024):
    t = x.shape[0]
    row = pl.BlockSpec((tm, D_MODEL), lambda i, k: (i, 0))
    full = lambda a: _layer_spec(a, layer)
    grp_in = pl.BlockSpec((EXPERTS_PER_GROUP, D_MODEL, D_EXPERT), lambda i, k: (layer * N_GROUPS + k, 0, 0))
    grp_out = pl.BlockSpec((None, EXPERTS_PER_GROUP * D_EXPERT, D_MODEL), lambda i, k: (layer * N_GROUPS + k, 0, 0))
    return pl.pallas_call(
        functools.partial(_moe_kernel, final=final),
        grid=(t // tm, N_GROUPS),
        in_specs=[row, full(g_ffn), full(wr), full(br), grp_in, grp_in, grp_out,
                  pl.BlockSpec(g_final.shape, lambda i, k: (0, 0))],
        out_specs=row,
        out_shape=jax.ShapeDtypeStruct((t, D_MODEL), F32),
        scratch_shapes=[pltpu.VMEM((tm, D_MODEL), BF16), pltpu.VMEM((N_GROUPS, tm, 2 * LANES), BF16),
                        pltpu.VMEM((1, tm), F32), pltpu.VMEM((tm, LANES), F32), pltpu.SMEM((N_GROUPS,), jnp.int32),
                        pltpu.VMEM((tm, D_MODEL), F32)],
        compiler_params=_cparams("parallel", "arbitrary"),
        name="moe",
    )(x, g_ffn, wr, br, wg, wu, wd, g_final)


def _pad_heads(w, per_head, keep):
    lead = w.shape[:-1]
    w = w.reshape(*lead, N_HEADS, per_head)[..., keep]
    w = jnp.pad(w, [(0, 0)] * (w.ndim - 1) + [(0, LANES - w.shape[-1])])
    return w.reshape(*lead, N_HEADS * LANES)


def _prepare_weights(w_in, w_uq, w_ukv, w_branch, w_out, w_group_router, b_group_router, w_expert_router,
                     b_expert_router, w_gate, w_up, w_down):
    depth = w_in.shape[0]
    w = jnp.pad(w_in.astype(BF16), ((0, 0), (0, 0), (0, -w_in.shape[2] % LANES)))
    wuq = _pad_heads(w_uq, MLA_NOPE + MLA_ROPE, slice(None)).astype(BF16)
    wuk = _pad_heads(w_ukv, 2 * MLA_NOPE, slice(0, MLA_NOPE)).astype(BF16)
    wuv = _pad_heads(w_ukv, 2 * MLA_NOPE, slice(MLA_NOPE, None)).astype(BF16)
    wa = w_branch[:, 0].astype(BF16)
    wb = w_branch[:, 1].reshape(depth, N_HEADS, HEAD_DIM, D_MODEL)
    wb = jnp.pad(wb, ((0, 0), (0, 0), (0, LANES - HEAD_DIM), (0, 0))).reshape(depth, N_HEADS * LANES, D_MODEL)
    wb = wb.astype(BF16)
    wc = w_branch[:, 2].astype(BF16)
    wo = w_out.astype(BF16)
    n_exp = N_GROUPS * EXPERTS_PER_GROUP
    wr = jnp.concatenate([w_group_router, w_expert_router], axis=2).transpose(0, 2, 1)
    wr = jnp.pad(wr, ((0, 0), (0, ROUTER_ROWS - N_GROUPS - n_exp), (0, 0)))
    wr_hi = wr.astype(BF16)
    wr = jnp.concatenate([wr_hi, (wr - wr_hi.astype(F32)).astype(BF16)], axis=1)
    br = jnp.concatenate([b_group_router, b_expert_router], axis=1)
    br = jnp.pad(br, ((0, 0), (0, ROUTER_ROWS - N_GROUPS - n_exp)))[:, :, None]

    wgate = w_gate.astype(BF16).reshape(depth * n_exp, D_MODEL, D_EXPERT)
    wup = w_up.astype(BF16).reshape(depth * n_exp, D_MODEL, D_EXPERT)
    wdown = w_down.astype(BF16).reshape(depth * N_GROUPS, EXPERTS_PER_GROUP * D_EXPERT, D_MODEL)
    return dict(w=w, wuq=wuq, wuk=wuk, wuv=wuv, wa=wa, wb=wb, wc=wc, wo=wo,
                wr=wr, br=br, wgate=wgate, wup=wup, wdown=wdown)


def kernel(x, positions, g_mix, w_in, g_q_lat, w_uq, g_kv_lat, w_ukv, w_branch, w_out, g_ffn, w_group_router, b_group_router, w_expert_router, b_expert_router, w_gate, w_up, w_down, g_final):
    b, s, d = x.shape
    depth = w_in.shape[0]
    assert d == D_MODEL and w_in.shape[2] == SB_COLS + MLA_COLS + DIL_COLS + GATE_COLS
    t = b * s
    w = _prepare_weights(w_in, w_uq, w_ukv, w_branch, w_out, w_group_router, b_group_router, w_expert_router,
                         b_expert_router, w_gate, w_up, w_down)
    tabs = _rope_tables(positions.reshape(t, 1).astype(F32))
    xf = x.reshape(t, d)
    g_fin = g_final.reshape(1, d)
    gains = [g[:, None, :] for g in (g_mix, g_q_lat, g_kv_lat, g_ffn)]
    for l in range(depth):
        sb, m, dil, gates = _proj(xf, b, l, gains[0], w["w"], gains[1], w["wuq"], gains[2], w["wuk"], w["wuv"], tabs)
        oa = _sb_attention(sb.reshape(b, s, -1)).reshape(t, -1)
        ob = _mla_attention(m.reshape(b, s, -1)).reshape(t, -1)
        oc = _dil_attention(dil).reshape(t, -1)
        xf = _merge(xf, oa, ob, oc, gates, l, w["wa"], w["wb"], w["wc"], w["wo"])
        xf = _moe(xf, l, gains[3], w["wr"], w["br"], w["wgate"], w["wup"], w["wdown"], g_fin,
                  final=(l == depth - 1))
    return xf.reshape(b, s, d)
```
